```python
import math
import jax
import jax.numpy as jnp
from jax import lax
import numpy as np

D_MODEL = 2048
BATCH = 2
SEQ = 8192
DEPTH = 4
DEC_BATCH = 8
DEC_SEQ = 16
PAST_LEN = 4096

CHUNK = 64
N_MIXERS = 3
ALPHA = (2.0 * DEPTH) ** 0.25
BETA = (8.0 * DEPTH) ** -0.25
LN_EPS = 1e-5
CC_WIDTH = 31
SB_HEADS = 16
SB_HEAD_DIM = D_MODEL // SB_HEADS
SB_QBLOCK = 128
SSM_EXPAND = 2
SSM_D_INNER = SSM_EXPAND * D_MODEL
SSM_HEAD_DIM = 64
SSM_HEADS = SSM_D_INNER // SSM_HEAD_DIM
SSM_GROUPS = 8
SSM_D_STATE = 128
SSM_CONV = 4
SSM_CONV_DIM = SSM_D_INNER + 2 * SSM_GROUPS * SSM_D_STATE
SSM_IN_DIM = SSM_D_INNER + SSM_CONV_DIM + SSM_HEADS
FFN_DIM = 5632
N_EXPERTS = 8
TOP_K = 2
EXPERT_DIM = 7168
N_CC_LAYERS = (DEPTH + 2) // 3
N_SB_LAYERS = (DEPTH + 1) // 3
N_SSM_LAYERS = DEPTH // 3
N_DENSE_LAYERS = (DEPTH + 1) // 2
N_MOE_LAYERS = DEPTH // 2

kernel_name = 'hybrid_streaming_encoder_step'


def layer_norm(x, g, b):
    xf = x.astype(jnp.float32)
    mu = jnp.mean(xf, axis=-1, keepdims=True)
    var = jnp.mean(jnp.square(xf - mu), axis=-1, keepdims=True)
    return ((xf - mu) * lax.rsqrt(var + LN_EPS) * g + b).astype(x.dtype)


def causal_dwconv(x, hist, w, b):
    k = w.shape[0]
    xp = jnp.concatenate([hist.astype(x.dtype), x], axis=1)
    y = lax.conv_general_dilated(xp, w[:, None, :].astype(x.dtype), window_strides=(1,), padding='VALID',
                                 dimension_numbers=('NWC', 'WIO', 'NWC'), feature_group_count=x.shape[-1])
    return y + b, xp[:, xp.shape[1] - (k - 1):]


def conformer_conv(x, hist, w_pw1, b_pw1, w_dw, b_dw, ln_g, ln_b, w_pw2, b_pw2):
    a, g = jnp.split(x @ w_pw1 + b_pw1, 2, axis=-1)
    u = a * jax.nn.sigmoid(g)
    c, new_hist = causal_dwconv(u, hist, w_dw, b_dw)
    c = jax.nn.silu(layer_norm(c, ln_g, ln_b))
    return c @ w_pw2 + b_pw2, new_hist


def sb_attend(q, k, v, q_pos, k_pos):
    z = jnp.einsum('bqhd,bkhd->bhqk', q, k).astype(jnp.float32) * (SB_HEAD_DIM ** -0.5)
    earlier = k_pos[None, :] < q_pos[:, None]
    log_keep = jnp.where(earlier, jax.nn.log_sigmoid(-z), 0.0)
    log_surv = lax.cumsum(log_keep, axis=3, reverse=True) - log_keep
    w = jnp.where(earlier, jnp.exp(jax.nn.log_sigmoid(z) + log_surv), 0.0)
    return jnp.einsum('bhqk,bkhd->bqhd', w.astype(v.dtype), v)


def stick_breaking(x, past_k, past_v, past_len, w_qkv, w_o):
    bsz, L, _ = x.shape
    qkv = (x @ w_qkv).reshape(bsz, L, 3, SB_HEADS, SB_HEAD_DIM)
    q, k, v = qkv[:, :, 0], qkv[:, :, 1], qkv[:, :, 2]
    if past_k is None:
        keys, vals = k, v
    else:
        keys = jnp.concatenate([past_k.astype(k.dtype), k], axis=1)
        vals = jnp.concatenate([past_v.astype(v.dtype), v], axis=1)
    k_pos = jnp.arange(keys.shape[1])
    q_pos = past_len + jnp.arange(L)
    if L % SB_QBLOCK == 0:
        nb = L // SB_QBLOCK
        qb = jnp.swapaxes(q.reshape(bsz, nb, SB_QBLOCK, SB_HEADS, SB_HEAD_DIM), 0, 1)
        pb = q_pos.reshape(nb, SB_QBLOCK)
        ob = lax.map(lambda a: sb_attend(a[0], keys, vals, a[1], k_pos), (qb, pb))
        o = jnp.swapaxes(ob, 0, 1)
    else:
        o = sb_attend(q, keys, vals, q_pos, k_pos)
    return o.reshape(bsz, L, D_MODEL) @ w_o, k, v


def ssd_scan(x, dt, A, Bm, Cm, state0, chunk):
    b, L, h, p = x.shape
    g = Bm.shape[2]
    r = h // g
    nc = L // chunk
    f32 = jnp.float32

    def to_chunks(t):
        return jnp.swapaxes(t.reshape((b, nc, chunk) + t.shape[2:]), 0, 1)

    xs = to_chunks(x.astype(f32))
    dts = to_chunks(dt)
    las = to_chunks(dt * A)
    Bs = to_chunks(Bm.astype(f32))
    Cs = to_chunks(Cm.astype(f32))
    causal = jnp.tril(jnp.ones((chunk, chunk), dtype=bool))

    def step(state, inp):
        xc, dtc, lac, Bc, Cc = inp
        cum = jnp.cumsum(lac, axis=1)
        seg = cum[:, :, None, :] - cum[:, None, :, :]
        decay = jnp.exp(jnp.where(causal[None, :, :, None], seg, -jnp.inf))
        cb = jnp.repeat(jnp.einsum('btgn,bsgn->btsg', Cc, Bc), r, axis=-1)
        y = jnp.einsum('btsh,bshp->bthp', cb * decay * dtc[:, None, :, :], xc)
        ch = jnp.repeat(Cc, r, axis=2)
        y = y + jnp.einsum('bthn,bhpn->bthp', ch, state) * jnp.exp(cum)[..., None]
        bh = jnp.repeat(Bc, r, axis=2)
        w_end = jnp.exp(cum[:, -1:, :] - cum) * dtc
        state = state * jnp.exp(cum[:, -1, :])[:, :, None, None] + jnp.einsum('bshn,bsh,bshp->bhpn', bh, w_end, xc)
        return state, y

    final, ys = lax.scan(step, state0.astype(f32), (xs, dts, las, Bs, Cs))
    return jnp.swapaxes(ys, 0, 1).reshape(b, L, h, p), final


def mamba2(x, conv_hist, ssm_state, w_in, conv_w, conv_b, dt_bias, A_log, D_skip, norm_w, w_out, chunk):
    bsz, L, _ = x.shape
    proj = x @ w_in
    z = proj[..., :SSM_D_INNER]
    xbc = proj[..., SSM_D_INNER:SSM_D_INNER + SSM_CONV_DIM]
    dt = proj[..., SSM_D_INNER + SSM_CONV_DIM:]
    xbc, new_conv = causal_dwconv(xbc, conv_hist, conv_w, conv_b)
    xbc = jax.nn.silu(xbc)
    gn = SSM_GROUPS * SSM_D_STATE
    xs = xbc[..., :SSM_D_INNER].reshape(bsz, L, SSM_HEADS, SSM_HEAD_DIM)
    Bm = xbc[..., SSM_D_INNER:SSM_D_INNER + gn].reshape(bsz, L, SSM_GROUPS, SSM_D_STATE)
    Cm = xbc[..., SSM_D_INNER + gn:].reshape(bsz, L, SSM_GROUPS, SSM_D_STATE)
    dt = jax.nn.softplus(dt.astype(jnp.float32) + dt_bias.astype(jnp.float32))
    A = -jnp.exp(A_log.astype(jnp.float32))
    y, new_state = ssd_scan(xs, dt, A, Bm, Cm, ssm_state, chunk)
    y = y + xs.astype(jnp.float32) * D_skip.astype(jnp.float32)[:, None]
    y = (y.reshape(bsz, L, SSM_D_INNER) * jax.nn.silu(z.astype(jnp.float32)))
    y = y.reshape(bsz, L, SSM_GROUPS, SSM_D_INNER // SSM_GROUPS)
    y = y * lax.rsqrt(jnp.mean(jnp.square(y), axis=-1, keepdims=True) + LN_EPS)
    y = (y.reshape(bsz, L, SSM_D_INNER) * norm_w).astype(x.dtype)
    return y @ w_out, new_conv, new_state


def swiglu(x, w_gu, w_down):
    g, u = jnp.split(x @ w_gu, 2, axis=-1)
    return (jax.nn.silu(g) * u) @ w_down


def moe(x, w_router, w_gu, w_down):
    logits = (x @ w_router).astype(jnp.float32)
    top_vals, top_idx = lax.top_k(logits, TOP_K)
    gates = jax.nn.softmax(top_vals, axis=-1)
    combine = jnp.sum(jax.nn.one_hot(top_idx, N_EXPERTS, dtype=jnp.float32) * gates[..., None], axis=-2)
    out = jnp.zeros(x.shape, jnp.float32)
    for e in range(N_EXPERTS):
        out = out + combine[..., e:e + 1] * swiglu(x, w_gu[e], w_down[e])
    return out.astype(x.dtype)


def trunk(x, cc_hist, sb_k_past, sb_v_past, ssm_conv_hist, ssm_state, past_len, p):
    L = x.shape[1]
    ssm_chunk = CHUNK if L % CHUNK == 0 else L
    new_cc, new_k, new_v, new_sc, new_ss = [], [], [], [], []
    for i in range(DEPTH):
        j = i // N_MIXERS
        kind = i % N_MIXERS
        if kind == 0:
            h, st = conformer_conv(x, cc_hist[j], p['cc_w_pw1'][j], p['cc_b_pw1'][j], p['cc_w_dw'][j],
                                   p['cc_b_dw'][j], p['cc_ln_g'][j], p['cc_ln_b'][j], p['cc_w_pw2'][j],
                                   p['cc_b_pw2'][j])
            new_cc.append(st)
        elif kind == 1:
            pk = None if sb_k_past is None else sb_k_past[j]
            pv = None if sb_v_past is None else sb_v_past[j]
            h, k, v = stick_breaking(x, pk, pv, past_len, p['sb_w_qkv'][j], p['sb_w_o'][j])
            new_k.append(k)
            new_v.append(v)
        else:
            h, sc, ss = mamba2(x, ssm_conv_hist[j], ssm_state[j], p['ssm_w_in'][j], p['ssm_conv_w'][j],
                               p['ssm_conv_b'][j], p['ssm_dt_bias'][j], p['ssm_A_log'][j], p['ssm_D'][j],
                               p['ssm_norm_w'][j], p['ssm_w_out'][j], ssm_chunk)
            new_sc.append(sc)
            new_ss.append(ss)
        x = layer_norm(ALPHA * x + h, p['ln_g'][i, 0], p['ln_b'][i, 0])
        f = i // 2
        if i % 2 == 0:
            h = swiglu(x, p['ffn_w_gu'][f], p['ffn_w_down'][f])
        else:
            h = moe(x, p['moe_w_router'][f], p['moe_w_gu'][f], p['moe_w_down'][f])
        x = layer_norm(ALPHA * x + h, p['ln_g'][i, 1], p['ln_b'][i, 1])
    return x, jnp.stack(new_cc), jnp.stack(new_k), jnp.stack(new_v), jnp.stack(new_sc), jnp.stack(new_ss)


def setup_inputs(seed: int = 0) -> dict:
    key = jax.random.key(seed)
    keys = jax.random.split(key, 40)
    counter = [0]

    def nxt():
        kk = keys[counter[0]]
        counter[0] += 1
        return kk

    def nrm(shape, scale):
        return jax.random.normal(nxt(), shape, jnp.float32) * scale

    D = D_MODEL
    inp = {}
    inp['x_prompt'] = nrm((BATCH, SEQ, D), 1.0)
    inp['x_sample'] = nrm((DEC_BATCH, DEC_SEQ, D), 1.0)
    inp['state_cconv'] = nrm((N_CC_LAYERS, DEC_BATCH, CC_WIDTH - 1, D), 0.5)
    inp['cache_sb_k'] = nrm((N_SB_LAYERS, DEC_BATCH, PAST_LEN, SB_HEADS, SB_HEAD_DIM), 1.0)
    inp['cache_sb_v'] = nrm((N_SB_LAYERS, DEC_BATCH, PAST_LEN, SB_HEADS, SB_HEAD_DIM), 1.0)
    inp['state_ssm_conv'] = nrm((N_SSM_LAYERS, DEC_BATCH, SSM_CONV - 1, SSM_CONV_DIM), 1.0)
    inp['state_ssm'] = nrm((N_SSM_LAYERS, DEC_BATCH, SSM_HEADS, SSM_HEAD_DIM, SSM_D_STATE), 0.1)
    inp['cc_w_pw1'] = nrm((N_CC_LAYERS, D, 2 * D), D ** -0.5)
    inp['cc_b_pw1'] = nrm((N_CC_LAYERS, 2 * D), 0.01)
    inp['cc_w_dw'] = nrm((N_CC_LAYERS, CC_WIDTH, D), CC_WIDTH ** -0.5)
    inp['cc_b_dw'] = nrm((N_CC_LAYERS, D), 0.01)
    inp['cc_ln_g'] = 1.0 + nrm((N_CC_LAYERS, D), 0.01)
    inp['cc_ln_b'] = nrm((N_CC_LAYERS, D), 0.01)
    inp['cc_w_pw2'] = nrm((N_CC_LAYERS, D, D), BETA * D ** -0.5)
    inp['cc_b_pw2'] = nrm((N_CC_LAYERS, D), 0.01)
    inp['sb_w_qkv'] = nrm((N_SB_LAYERS, D, 3 * D), D ** -0.5)
    inp['sb_w_o'] = nrm((N_SB_LAYERS, D, D), BETA * D ** -0.5)
    inp['ssm_w_in'] = nrm((N_SSM_LAYERS, D, SSM_IN_DIM), D ** -0.5)
    inp['ssm_conv_w'] = nrm((N_SSM_LAYERS, SSM_CONV, SSM_CONV_DIM), SSM_CONV ** -0.5)
    inp['ssm_conv_b'] = nrm((N_SSM_LAYERS, SSM_CONV_DIM), 0.01)
    dt0 = jnp.exp(jax.random.uniform(nxt(), (N_SSM_LAYERS, SSM_HEADS), jnp.float32, math.log(1e-3), math.log(1e-1)))
    inp['ssm_dt_bias'] = dt0 + jnp.log(-jnp.expm1(-dt0))
    inp['ssm_A_log'] = jnp.log(jax.random.uniform(nxt(), (N_SSM_LAYERS, SSM_HEADS), jnp.float32, 1.0, 16.0))
    inp['ssm_D'] = 1.0 + nrm((N_SSM_LAYERS, SSM_HEADS), 0.01)
    inp['ssm_norm_w'] = 1.0 + nrm((N_SSM_LAYERS, SSM_D_INNER), 0.01)
    inp['ssm_w_out'] = nrm((N_SSM_LAYERS, SSM_D_INNER, D), BETA * SSM_D_INNER ** -0.5)
    inp['ffn_w_gu'] = nrm((N_DENSE_LAYERS, D, 2 * FFN_DIM), D ** -0.5)
    inp['ffn_w_down'] = nrm((N_DENSE_LAYERS, FFN_DIM, D), BETA * FFN_DIM ** -0.5)
    inp['moe_w_router'] = nrm((N_MOE_LAYERS, D, N_EXPERTS), D ** -0.5)
    inp['moe_w_gu'] = nrm((N_MOE_LAYERS, N_EXPERTS, D, 2 * EXPERT_DIM), D ** -0.5)
    inp['moe_w_down'] = nrm((N_MOE_LAYERS, N_EXPERTS, EXPERT_DIM, D), BETA * EXPERT_DIM ** -0.5)
    inp['ln_g'] = 1.0 + nrm((DEPTH, 2, D), 0.01)
    inp['ln_b'] = nrm((DEPTH, 2, D), 0.01)
    return inp


def reference(x_prompt, x_sample, state_cconv, cache_sb_k, cache_sb_v, state_ssm_conv, state_ssm,
              cc_w_pw1, cc_b_pw1, cc_w_dw, cc_b_dw, cc_ln_g, cc_ln_b, cc_w_pw2, cc_b_pw2,
              sb_w_qkv, sb_w_o,
              ssm_w_in, ssm_conv_w, ssm_conv_b, ssm_dt_bias, ssm_A_log, ssm_D, ssm_norm_w, ssm_w_out,
              ffn_w_gu, ffn_w_down, moe_w_router, moe_w_gu, moe_w_down, ln_g, ln_b):
    p = dict(cc_w_pw1=cc_w_pw1, cc_b_pw1=cc_b_pw1, cc_w_dw=cc_w_dw, cc_b_dw=cc_b_dw, cc_ln_g=cc_ln_g,
             cc_ln_b=cc_ln_b, cc_w_pw2=cc_w_pw2, cc_b_pw2=cc_b_pw2, sb_w_qkv=sb_w_qkv, sb_w_o=sb_w_o,
             ssm_w_in=ssm_w_in, ssm_conv_w=ssm_conv_w, ssm_conv_b=ssm_conv_b, ssm_dt_bias=ssm_dt_bias,
             ssm_A_log=ssm_A_log, ssm_D=ssm_D, ssm_norm_w=ssm_norm_w, ssm_w_out=ssm_w_out,
             ffn_w_gu=ffn_w_gu, ffn_w_down=ffn_w_down, moe_w_router=moe_w_router, moe_w_gu=moe_w_gu,
             moe_w_down=moe_w_down, ln_g=ln_g, ln_b=ln_b)
    bp = x_prompt.shape[0]
    cc0 = jnp.zeros((N_CC_LAYERS, bp, CC_WIDTH - 1, D_MODEL), x_prompt.dtype)
    sc0 = jnp.zeros((N_SSM_LAYERS, bp, SSM_CONV - 1, SSM_CONV_DIM), x_prompt.dtype)
    ss0 = jnp.zeros((N_SSM_LAYERS, bp, SSM_HEADS, SSM_HEAD_DIM, SSM_D_STATE), jnp.float32)
    y_prompt, cc_p, k_p, v_p, sc_p, ss_p = trunk(x_prompt, cc0, None, None, sc0, ss0, 0, p)
    past_len = cache_sb_k.shape[2]
    y_sample, cc_s, k_s, v_s, sc_s, ss_s = trunk(x_sample, state_cconv, cache_sb_k, cache_sb_v,
                                                 state_ssm_conv, state_ssm, past_len, p)
    return (y_prompt, y_sample, cc_p, cc_s, k_p, v_p, k_s, v_s, sc_p, sc_s, ss_p, ss_s)
```

```python
import functools
import math

import jax
import jax.numpy as jnp
from jax import lax
from jax.experimental import pallas as pl
from jax.experimental.pallas import tpu as pltpu

F32 = jnp.float32
BF16 = jnp.bfloat16

D_MODEL = 2048
DEPTH = 4
ALPHA = (2.0 * DEPTH) ** 0.25
LN_EPS = 1e-5
CC_WIDTH = 31
SB_HEADS = 16
SB_HEAD_DIM = 128
SSM_D_INNER = 4096
SSM_HEAD_DIM = 64
SSM_HEADS = 64
SSM_GROUPS = 8
SSM_D_STATE = 128
SSM_CONV = 4
SSM_CONV_DIM = SSM_D_INNER + 2 * SSM_GROUPS * SSM_D_STATE
FFN_DIM = 5632
N_EXPERTS = 8
EXPERT_DIM = 7168

VMEM_LIMIT_BYTES = 56 * 1024 * 1024
LANES = 128
SUBLANES = 8


def _cparams(sem):
    return pltpu.CompilerParams(dimension_semantics=sem, vmem_limit_bytes=VMEM_LIMIT_BYTES)


def _bdot(a, b):
    return jnp.dot(a.astype(BF16), b.astype(BF16), preferred_element_type=F32)


def _split_dot(a, onehot_bf16, passes):
    out = None
    rem = a
    for p in range(passes):
        piece = rem.astype(BF16)
        term = jnp.dot(piece, onehot_bf16, preferred_element_type=F32)
        out = term if out is None else out + term
        if p + 1 < passes:
            rem = rem - piece.astype(F32)
    return out


def _split_dot_left(onehot_bf16, a, passes):
    out = None
    rem = a
    for p in range(passes):
        piece = rem.astype(BF16)
        term = jnp.dot(onehot_bf16, piece, preferred_element_type=F32)
        out = term if out is None else out + term
        if p + 1 < passes:
            rem = rem - piece.astype(F32)
    return out


def _layer_norm_rows(y, g, b):
    mu = jnp.mean(y, axis=-1, keepdims=True)
    yc = y - mu
    var = jnp.mean(yc * yc, axis=-1, keepdims=True)
    return yc * lax.rsqrt(var + LN_EPS) * g + b


def _sigmoid(x):
    return 1.0 / (1.0 + jnp.exp(-x))


def _silu(x):
    return x * _sigmoid(x)


def _softplus(x):
    return jnp.maximum(x, 0.0) + jnp.log(1.0 + jnp.exp(-jnp.abs(x)))


def _ws_kernel(te_ref, nu_ref, *refs, act, has_bias, n_outs):
    del te_ref
    i = pl.program_id(1)
    outs = refs[len(refs) - n_outs:]
    ins = refs[:len(refs) - n_outs]
    x_ref = ins[0]
    two = act in ("glu", "swiglu")

    @pl.when(i < nu_ref[0])
    def _():
        x = x_ref[...].astype(BF16)
        a = jnp.dot(x, ins[1][0], preferred_element_type=F32)
        if two:
            b = jnp.dot(x, ins[2][0], preferred_element_type=F32)
        if has_bias:
            a = a + ins[3 if two else 2][0]
            if two:
                b = b + ins[4][0]
        if act == "glu":
            r = a * _sigmoid(b)
        elif act == "swiglu":
            r = _silu(a) * b
        else:
            r = a
        for o in outs:
            o[...] = r.astype(o.dtype)

    @pl.when(i >= nu_ref[0])
    def _():
        for o in outs:
            o[...] = jnp.zeros(o.shape, o.dtype)


def ws_matmul(x, w, *, n_out, tm, tn, act="none", bias=None, tile_expert=None, n_used=None,
              out_dtypes=(F32,)):
    rows, kdim = x.shape
    assert rows % tm == 0 and n_out % tn == 0
    n_row_tiles = rows // tm
    n_col_tiles = n_out // tn
    two = act in ("glu", "swiglu")
    if tile_expert is None:
        tile_expert = jnp.zeros((n_row_tiles,), jnp.int32)
        n_used = jnp.full((1,), n_row_tiles, jnp.int32)
    in_specs = [pl.BlockSpec((tm, kdim), lambda j, i, te, nu: (i, 0)),
                pl.BlockSpec((1, kdim, tn), lambda j, i, te, nu: (te[i], 0, j))]
    args = [x, w]
    if two:
        in_specs.append(pl.BlockSpec((1, kdim, tn), lambda j, i, te, nu: (te[i], 0, j + n_col_tiles)))
        args.append(w)
    if bias is not None:
        in_specs.append(pl.BlockSpec((1, 1, tn), lambda j, i, te, nu: (te[i], 0, j)))
        args.append(bias)
        if two:
            in_specs.append(pl.BlockSpec((1, 1, tn), lambda j, i, te, nu: (te[i], 0, j + n_col_tiles)))
            args.append(bias)
    out_specs = [pl.BlockSpec((tm, tn), lambda j, i, te, nu: (i, j)) for _ in out_dtypes]
    out_shape = [jax.ShapeDtypeStruct((rows, n_out), dt) for dt in out_dtypes]
    res = pl.pallas_call(
        functools.partial(_ws_kernel, act=act, has_bias=bias is not None, n_outs=len(out_dtypes)),
        grid_spec=pltpu.PrefetchScalarGridSpec(
            num_scalar_prefetch=2, grid=(n_col_tiles, n_row_tiles),
            in_specs=in_specs, out_specs=out_specs),
        out_shape=out_shape,
        compiler_params=_cparams(("arbitrary", "arbitrary")),
    )(tile_expert, n_used, *args)
    return res


def _xs_ln_kernel(x_ref, w_ref, bias_ref, res_ref, g_ref, b_ref, of_ref, ob_ref, acc_ref, *, nk):
    k = pl.program_id(1)

    @pl.when(k == 0)
    def _():
        acc_ref[...] = jnp.zeros(acc_ref.shape, F32)

    acc_ref[...] += jnp.dot(x_ref[...].astype(BF16), w_ref[...], preferred_element_type=F32)

    @pl.when(k == nk - 1)
    def _():
        y = ALPHA * res_ref[...] + (acc_ref[...] + bias_ref[...])
        out = _layer_norm_rows(y, g_ref[...], b_ref[...])
        of_ref[...] = out
        ob_ref[...] = out.astype(BF16)


def xs_matmul_ln(x, w, bias, res, g, b, *, tm, tk):
    rows, kdim = x.shape
    n = w.shape[1]
    assert rows % tm == 0 and kdim % tk == 0 and n == D_MODEL
    nk = kdim // tk
    row = lambda a: a.reshape(1, n).astype(F32)
    return pl.pallas_call(
        functools.partial(_xs_ln_kernel, nk=nk),
        grid=(rows // tm, nk),
        in_specs=[pl.BlockSpec((tm, tk), lambda i, k: (i, k)),
                  pl.BlockSpec((tk, n), lambda i, k: (k, 0)),
                  pl.BlockSpec((1, n), lambda i, k: (0, 0)),
                  pl.BlockSpec((tm, n), lambda i, k: (i, 0)),
                  pl.BlockSpec((1, n), lambda i, k: (0, 0)),
                  pl.BlockSpec((1, n), lambda i, k: (0, 0))],
        out_specs=[pl.BlockSpec((tm, n), lambda i, k: (i, 0)),
                   pl.BlockSpec((tm, n), lambda i, k: (i, 0))],
        out_shape=[jax.ShapeDtypeStruct((rows, n), F32), jax.ShapeDtypeStruct((rows, n), BF16)],
        scratch_shapes=[pltpu.VMEM((tm, n), F32)],
        compiler_params=_cparams(("arbitrary", "arbitrary")),
    )(x, w, row(bias), res, row(g), row(b))


CONV_ROW_CHUNK = 32
CONV_LANE_CHUNK = 512


def _dwconv_kernel(hist_ref, prev_ref, cur_ref, w_ref, cb_ref, g_ref, b_ref, o_ref, win_ref, y_ref,
                   *, taps, halo, tl, mode):
    i = pl.program_id(1)

    @pl.when(i == 0)
    def _():
        win_ref[0:halo, :] = hist_ref[0]

    @pl.when(i > 0)
    def _():
        win_ref[0:halo, :] = prev_ref[0]

    win_ref[halo:halo + tl, :] = cur_ref[0]
    off = halo - (taps - 1)
    width = cur_ref.shape[2]
    rc = min(CONV_ROW_CHUNK, tl)
    lc = min(CONV_LANE_CHUNK, width)
    for r0 in range(0, tl, rc):
        for c0 in range(0, width, lc):
            acc = jnp.zeros((rc, lc), F32) + cb_ref[:, c0:c0 + lc]
            for k in range(taps):
                acc = acc + w_ref[k:k + 1, c0:c0 + lc] * win_ref[off + k + r0:off + k + r0 + rc, c0:c0 + lc]
            if mode == "silu":
                o_ref[0, r0:r0 + rc, c0:c0 + lc] = _silu(acc).astype(o_ref.dtype)
            else:
                y_ref[r0:r0 + rc, c0:c0 + lc] = acc
    if mode == "ln_silu":
        y = _layer_norm_rows(y_ref[...], g_ref[...], b_ref[...])
        o_ref[0] = _silu(y).astype(o_ref.dtype)


def dwconv(x, hist, w, cbias, ln_g, ln_b, *, taps, halo, tl, tc, mode, out_dtype):
    nseq, length, chans = x.shape
    assert length % tl == 0 and chans % tc == 0
    per = tl // halo if tl >= halo else 1
    if tl >= halo:
        assert tl % halo == 0
        prev_arr = x
        prev_spec = pl.BlockSpec((1, halo, tc), lambda s, i, c: (s, jnp.maximum(i * per - 1, 0), c))
    else:
        assert length == tl
        prev_arr = hist
        prev_spec = pl.BlockSpec((1, halo, tc), lambda s, i, c: (s, 0, c))
    row = lambda a: a.reshape(1, chans).astype(F32)
    if ln_g is None:
        ln_g = jnp.ones((chans,), F32)
        ln_b = jnp.zeros((chans,), F32)
    return pl.pallas_call(
        functools.partial(_dwconv_kernel, taps=taps, halo=halo, tl=tl, mode=mode),
        grid=(nseq, length // tl, chans // tc),
        in_specs=[pl.BlockSpec((1, halo, tc), lambda s, i, c: (s, 0, c)),
                  prev_spec,
                  pl.BlockSpec((1, tl, tc), lambda s, i, c: (s, i, c)),
                  pl.BlockSpec((taps, tc), lambda s, i, c: (0, c)),
                  pl.BlockSpec((1, tc), lambda s, i, c: (0, c)),
                  pl.BlockSpec((1, tc), lambda s, i, c: (0, c)),
                  pl.BlockSpec((1, tc), lambda s, i, c: (0, c))],
        out_specs=pl.BlockSpec((1, tl, tc), lambda s, i, c: (s, i, c)),
        out_shape=jax.ShapeDtypeStruct((nseq, length, chans), out_dtype),
        scratch_shapes=[pltpu.VMEM((halo + tl, tc), F32), pltpu.VMEM((tl, tc), F32)],
        compiler_params=_cparams(("arbitrary", "arbitrary", "arbitrary")),
    )(hist, prev_arr, x, w.astype(F32), row(cbias), row(ln_g), row(ln_b))


SB_SUB = 256


def _sb_block(q, kb, vb, tri, carry, mask):
    z = lax.dot_general(q, kb, (((1,), (1,)), ((), ())), preferred_element_type=F32)
    z = z * (SB_HEAD_DIM ** -0.5)
    sp = _softplus(z)
    log_keep = -sp
    if mask is not None:
        log_keep = jnp.where(mask, log_keep, 0.0)
    suffix = _split_dot(log_keep, tri, 2)
    w = jnp.exp(z - sp + suffix + carry)
    if mask is not None:
        w = jnp.where(mask, w, 0.0)
    contrib = jnp.dot(w.astype(BF16), vb, preferred_element_type=F32)
    new_carry = carry + suffix[:, 0:1] + log_keep[:, 0:1]
    return contrib, new_carry


def _sb_prompt_kernel(qi_ref, kj_ref, q_ref, k_ref, v_ref, tri_ref, o_ref, acc_ref, carry_ref, *, tq, tk):
    p = pl.program_id(2)
    qi = qi_ref[p]
    kj = kj_ref[p]

    @pl.when(kj == qi)
    def _():
        acc_ref[...] = jnp.zeros(acc_ref.shape, F32)
        carry_ref[...] = jnp.zeros(carry_ref.shape, F32)

    q = q_ref[...]
    tri = tri_ref[...]
    q_pos = qi * tq + lax.broadcasted_iota(jnp.int32, (tq, SB_SUB), 0)
    acc = acc_ref[...]
    carry = carry_ref[...]
    for s in range(tk // SB_SUB - 1, -1, -1):
        k_pos = kj * tk + s * SB_SUB + lax.broadcasted_iota(jnp.int32, (tq, SB_SUB), 1)
        contrib, carry = _sb_block(q, k_ref[s * SB_SUB:(s + 1) * SB_SUB, :], v_ref[s * SB_SUB:(s + 1) * SB_SUB, :],
                                   tri, carry, k_pos < q_pos)
        acc = acc + contrib
    acc_ref[...] = acc
    carry_ref[...] = carry

    @pl.when(kj == 0)
    def _():
        o_ref[...] = acc.astype(o_ref.dtype)


def sb_attention_prompt(qkv, tri, *, nbatch, seq, tq, tk):
    assert tq == tk and seq % tq == 0 and tk % SB_SUB == 0
    nq = seq // tq
    pairs = [(a, b) for a in range(nq) for b in range(a, -1, -1)]
    qi = jnp.asarray([a for a, _ in pairs], jnp.int32)
    kj = jnp.asarray([b for _, b in pairs], jnp.int32)
    h = SB_HEADS
    return pl.pallas_call(
        functools.partial(_sb_prompt_kernel, tq=tq, tk=tk),
        grid_spec=pltpu.PrefetchScalarGridSpec(
            num_scalar_prefetch=2, grid=(nbatch, h, len(pairs)),
            in_specs=[pl.BlockSpec((tq, SB_HEAD_DIM), lambda b, hh, p, qi, kj: (b * nq + qi[p], hh)),
                      pl.BlockSpec((tk, SB_HEAD_DIM), lambda b, hh, p, qi, kj: (b * nq + kj[p], h + hh)),
                      pl.BlockSpec((tk, SB_HEAD_DIM), lambda b, hh, p, qi, kj: (b * nq + kj[p], 2 * h + hh)),
                      pl.BlockSpec((SB_SUB, SB_SUB), lambda b, hh, p, qi, kj: (0, 0))],
            out_specs=pl.BlockSpec((tq, SB_HEAD_DIM), lambda b, hh, p, qi, kj: (b * nq + qi[p], hh)),
            scratch_shapes=[pltpu.VMEM((tq, SB_HEAD_DIM), F32), pltpu.VMEM((tq, 1), F32)]),
        out_shape=jax.ShapeDtypeStruct((nbatch * seq, D_MODEL), BF16),
        compiler_params=_cparams(("arbitrary", "arbitrary", "arbitrary")),
    )(qi, kj, qkv, qkv, qkv, tri)


def _sb_sample_kernel(q_ref, kn_ref, vn_ref, kp_ref, vp_ref, tri_ref, o_ref, acc_ref, carry_ref,
                      *, lq, tk, nsteps):
    b = pl.program_id(0)
    s = pl.program_id(2)
    q = q_ref[...]
    tri = tri_ref[...]

    @pl.when(s == 0)
    def _():
        n = kn_ref.shape[0]
        row = lax.broadcasted_iota(jnp.int32, (lq, n), 0)
        col = lax.broadcasted_iota(jnp.int32, (lq, n), 1)
        mask = (col >= b * lq) & (col < b * lq + row)
        contrib, carry = _sb_block(q, kn_ref[...], vn_ref[...], tri[0:n, 0:n], jnp.zeros((lq, 1), F32), mask)
        acc_ref[...] = contrib
        carry_ref[...] = carry

    @pl.when(s > 0)
    def _():
        acc = acc_ref[...]
        carry = carry_ref[...]
        for u in range(tk // SB_SUB - 1, -1, -1):
            kb = kp_ref[0, u * SB_SUB:(u + 1) * SB_SUB, :].astype(BF16)
            vb = vp_ref[0, u * SB_SUB:(u + 1) * SB_SUB, :].astype(BF16)
            contrib, carry = _sb_block(q, kb, vb, tri, carry, None)
            acc = acc + contrib
        acc_ref[...] = acc
        carry_ref[...] = carry

    @pl.when(s == nsteps - 1)
    def _():
        o_ref[...] = acc_ref[...].astype(o_ref.dtype)


def sb_attention_sample(qkv, past_k, past_v, tri, *, row0, nbatch, lq, tk):
    past = past_k.shape[1]
    nnew = nbatch * lq
    assert past % tk == 0 and row0 % nnew == 0 and row0 % lq == 0 and nnew <= SB_SUB
    nsteps = 1 + past // tk
    h = SB_HEADS
    nb = past // tk
    return pl.pallas_call(
        functools.partial(_sb_sample_kernel, lq=lq, tk=tk, nsteps=nsteps),
        grid=(nbatch, h, nsteps),
        in_specs=[pl.BlockSpec((lq, SB_HEAD_DIM), lambda b, hh, s: (row0 // lq + b, hh)),
                  pl.BlockSpec((nnew, SB_HEAD_DIM), lambda b, hh, s: (row0 // nnew, h + hh)),
                  pl.BlockSpec((nnew, SB_HEAD_DIM), lambda b, hh, s: (row0 // nnew, 2 * h + hh)),
                  pl.BlockSpec((1, tk, SB_HEAD_DIM), lambda b, hh, s: (b, jnp.minimum(nb - s, nb - 1), hh)),
                  pl.BlockSpec((1, tk, SB_HEAD_DIM), lambda b, hh, s: (b, jnp.minimum(nb - s, nb - 1), hh)),
                  pl.BlockSpec((SB_SUB, SB_SUB), lambda b, hh, s: (0, 0))],
        out_specs=pl.BlockSpec((lq, SB_HEAD_DIM), lambda b, hh, s: (b, hh)),
        out_shape=jax.ShapeDtypeStruct((nnew, D_MODEL), BF16),
        scratch_shapes=[pltpu.VMEM((lq, SB_HEAD_DIM), F32), pltpu.VMEM((lq, 1), F32)],
        compiler_params=_cparams(("arbitrary", "arbitrary", "arbitrary")),
    )(qkv, qkv, qkv, past_k, past_v, tri)


SSD_Q = 128
SSM_HPAD = 128


def _ssd_kernel(xbc_ref, z_ref, dt_ref, dtt_ref, st0_ref, dtb_ref, dtbc_ref, a_ref, ac_ref, dx_ref, nw_ref,
                e64_ref, eq_ref, lincl_ref, uincl_ref, y_ref, stout_ref, st_ref, yscr_ref, *, valid, nchunks):
    c = pl.program_id(1)
    q = SSD_Q
    hd = SSM_HEAD_DIM
    pair = 2 * hd

    @pl.when(c == 0)
    def _():
        st_ref[...] = st0_ref[0]

    row_t = lax.broadcasted_iota(jnp.int32, (q, SSM_HPAD), 0) + c * q
    dt = jnp.where(row_t < valid, _softplus(dt_ref[0] + dtb_ref[...]), 0.0)
    col_t = lax.broadcasted_iota(jnp.int32, (SSM_HPAD, q), 1) + c * q
    dtt = jnp.where(col_t < valid, _softplus(dtt_ref[0] + dtbc_ref[...]), 0.0)
    cum = _split_dot_left(lincl_ref[...], dt * a_ref[...], 3)
    cumt = _split_dot(dtt * ac_ref[...], uincl_ref[...], 3)
    e64 = e64_ref[...]
    dtx = _split_dot(dt, e64, 2)
    cumx = _split_dot(cum, e64, 3)
    colc = _split_dot(cum, eq_ref[...], 3)
    clx = cumx[q - 1:q, :]
    xs = xbc_ref[0, :, 0:SSM_D_INNER]
    xdt = xs * dtx
    xt = xdt.astype(BF16)
    xw = (xdt * jnp.exp(clx - cumx)).astype(BF16)
    ecum = jnp.exp(cumx)
    sdec = jnp.exp(clx)
    causal = lax.broadcasted_iota(jnp.int32, (q, q), 1) <= lax.broadcasted_iota(jnp.int32, (q, q), 0)
    lane_lo = lax.broadcasted_iota(jnp.int32, (q, pair), 1) < hd
    gn = SSM_GROUPS * SSM_D_STATE
    hpg = SSM_HEADS // SSM_GROUPS
    for g in range(SSM_GROUPS):
        bg_f = xbc_ref[0, :, SSM_D_INNER + g * SSM_D_STATE:SSM_D_INNER + (g + 1) * SSM_D_STATE]
        bg = bg_f.astype(BF16)
        bgt = bg_f.T.astype(BF16)
        cg =xbc_ref[0, :, SSM_D_INNER + gn + g * SSM_D_STATE:SSM_D_INNER + gn + (g + 1) * SSM_D_STATE].astype(BF16)
        cb = lax.dot_general(cg, bg, (((1,), (1,)), ((), ())), preferred_element_type=F32)
        for j in range(g * hpg // 2, (g + 1) * hpg // 2):
            lanes = slice(j * pair, (j + 1) * pair)
            ys = []
            for hh in range(2):
                h = 2 * j + hh
                seg = colc[:, h * q:(h + 1) * q] - cumt[h:h + 1, :]
                m = jnp.where(causal, cb * jnp.exp(seg), 0.0).astype(BF16)
                ys.append(jnp.dot(m, xt[:, lanes], preferred_element_type=F32))
            y = jnp.where(lane_lo, ys[0], ys[1])
            st = st_ref[:, lanes]
            y = y + jnp.dot(cg, st.astype(BF16), preferred_element_type=F32) * ecum[:, lanes]
            upd = jnp.dot(bgt, xw[:, lanes], preferred_element_type=F32)
            st_ref[:, lanes] = st * sdec[:, lanes] + upd
            y = y + xs[:, lanes] * dx_ref[:, lanes]
            yscr_ref[:, lanes] = y * _silu(z_ref[0, :, lanes])
    gw = SSM_D_INNER // SSM_GROUPS
    for g in range(SSM_GROUPS):
        blk = yscr_ref[:, g * gw:(g + 1) * gw]
        ms = jnp.mean(blk * blk, axis=-1, keepdims=True)
        y_ref[0, :, g * gw:(g + 1) * gw] = (blk * lax.rsqrt(ms + LN_EPS) * nw_ref[:, g * gw:(g + 1) * gw]).astype(y_ref.dtype)

    @pl.when(c == nchunks - 1)
    def _():
        stout_ref[0] = st_ref[...]


def ssd_scan(xbc, z, dt_raw, state0_t, dt_bias, a_log, d_skip, norm_w, *, valid):
    nseq, length, _ = xbc.shape
    q = SSD_Q
    assert length % q == 0
    nchunks = length // q
    hcount = SSM_HPAD
    assert dt_raw.shape[2] == hcount
    padh = lambda v: jnp.pad(v.astype(F32), (0, hcount - SSM_HEADS))
    dt_bias = padh(dt_bias)
    dtt_raw = jnp.swapaxes(dt_raw, 1, 2)
    a = padh(-jnp.exp(a_log.astype(F32)))
    heads = jnp.arange(hcount)
    e64 = (jnp.arange(SSM_D_INNER)[None, :] // SSM_HEAD_DIM == heads[:, None]).astype(BF16)
    eq = (jnp.arange(SSM_HEADS * q)[None, :] // q == heads[:, None]).astype(BF16)
    ti = jnp.arange(q)
    lincl = (ti[None, :] <= ti[:, None]).astype(BF16)
    uincl = (ti[:, None] <= ti[None, :]).astype(BF16)
    dx = jnp.repeat(d_skip.astype(F32), SSM_HEAD_DIM).reshape(1, SSM_D_INNER)
    full = lambda shape: pl.BlockSpec(shape, lambda s, c: tuple(0 for _ in shape))
    return pl.pallas_call(
        functools.partial(_ssd_kernel, valid=valid, nchunks=nchunks),
        grid=(nseq, nchunks),
        in_specs=[pl.BlockSpec((1, q, SSM_CONV_DIM), lambda s, c: (s, c, 0)),
                  pl.BlockSpec((1, q, SSM_D_INNER), lambda s, c: (s, c, 0)),
                  pl.BlockSpec((1, q, hcount), lambda s, c: (s, c, 0)),
                  pl.BlockSpec((1, hcount, q), lambda s, c: (s, 0, c)),
                  pl.BlockSpec((1, SSM_D_STATE, SSM_D_INNER), lambda s, c: (s, 0, 0)),
                  full((1, hcount)), full((hcount, 1)), full((1, hcount)), full((hcount, 1)),
                  full((1, SSM_D_INNER)), full((1, SSM_D_INNER)),
                  full((hcount, SSM_D_INNER)), full((hcount, SSM_HEADS * q)), full((q, q)), full((q, q))],
        out_specs=[pl.BlockSpec((1, q, SSM_D_INNER), lambda s, c: (s, c, 0)),
                   pl.BlockSpec((1, SSM_D_STATE, SSM_D_INNER), lambda s, c: (s, 0, 0))],
        out_shape=[jax.ShapeDtypeStruct((nseq, length, SSM_D_INNER), BF16),
                   jax.ShapeDtypeStruct((nseq, SSM_D_STATE, SSM_D_INNER), F32)],
        scratch_shapes=[pltpu.VMEM((SSM_D_STATE, SSM_D_INNER), F32), pltpu.VMEM((q, SSM_D_INNER), F32)],
        compiler_params=_cparams(("arbitrary", "arbitrary")),
    )(xbc, z, dt_raw, dtt_raw, state0_t, dt_bias.reshape(1, hcount).astype(F32),
      dt_bias.reshape(hcount, 1).astype(F32), a.reshape(1, hcount), a.reshape(hcount, 1), dx,
      norm_w.reshape(1, SSM_D_INNER).astype(F32), e64, eq, lincl, uincl)


META_E1, META_E2, META_G1, META_G2, META_R1, META_R2 = range(6)


def _router_kernel(x_ref, w_ref, meta_ref, cnt_ref, carry_ref, *, tm, nsteps):
    i = pl.program_id(0)

    @pl.when(i == 0)
    def _():
        carry_ref[...] = jnp.zeros(carry_ref.shape, F32)

    logits = jnp.dot(x_ref[...], w_ref[...], preferred_element_type=F32, precision=lax.Precision.HIGHEST)
    lane = lax.broadcasted_iota(jnp.int32, (tm, LANES), 1).astype(F32)
    logits = jnp.where(lane < N_EXPERTS, logits, -jnp.inf)
    m1 = jnp.max(logits, axis=-1, keepdims=True)
    i1 = jnp.min(jnp.where(logits == m1, lane, float(LANES)), axis=-1, keepdims=True)
    rest = jnp.where(lane == i1, -jnp.inf, logits)
    m2 = jnp.max(rest, axis=-1, keepdims=True)
    i2 = jnp.min(jnp.where(rest == m2, lane, float(LANES)), axis=-1, keepdims=True)
    e2 = jnp.exp(m2 - m1)
    g1 = 1.0 / (1.0 + e2)
    g2 = e2 / (1.0 + e2)
    oh1 = (lane == i1).astype(F32)
    oh2 = (lane == i2).astype(F32)
    both = oh1 + oh2
    r = lax.broadcasted_iota(jnp.int32, (tm, tm), 0)
    cc = lax.broadcasted_iota(jnp.int32, (tm, tm), 1)
    strict = (cc < r).astype(BF16)
    before = jnp.dot(strict, both.astype(BF16), preferred_element_type=F32) + carry_ref[0:1, :]
    r1 = jnp.sum(oh1 * before, axis=-1, keepdims=True)
    r2 = jnp.sum(oh2 * before, axis=-1, keepdims=True)
    carry_ref[0:1, :] = carry_ref[0:1, :] + jnp.sum(both, axis=0, keepdims=True)
    meta = jnp.zeros((tm, LANES), F32)
    for idx, val in ((META_E1, i1), (META_E2, i2), (META_G1, g1), (META_G2, g2), (META_R1, r1), (META_R2, r2)):
        meta = jnp.where(lane == float(idx), val, meta)
    meta_ref[...] = meta

    @pl.when(i == nsteps - 1)
    def _():
        cnt_ref[...] = carry_ref[...]


def moe_router(x, w_router, *, tm):
    rows = x.shape[0]
    assert rows % tm == 0
    nsteps = rows // tm
    wpad = jnp.zeros((D_MODEL, LANES), F32).at[:, :N_EXPERTS].set(w_router.astype(F32))
    return pl.pallas_call(
        functools.partial(_router_kernel, tm=tm, nsteps=nsteps),
        grid=(nsteps,),
        in_specs=[pl.BlockSpec((tm, D_MODEL), lambda i: (i, 0)),
                  pl.BlockSpec((D_MODEL, LANES), lambda i: (0, 0))],
        out_specs=[pl.BlockSpec((tm, LANES), lambda i: (i, 0)),
                   pl.BlockSpec((SUBLANES, LANES), lambda i: (0, 0))],
        out_shape=[jax.ShapeDtypeStruct((rows, LANES), F32), jax.ShapeDtypeStruct((SUBLANES, LANES), F32)],
        scratch_shapes=[pltpu.VMEM((SUBLANES, LANES), F32)],
        compiler_params=_cparams(("arbitrary",)),
    )(x, wpad)


ROW_SLAB = (D_MODEL // LANES, LANES)


def _scatter_kernel(d1_ref, d2_ref, x_ref, init_ref, o_ref, sem, *, tm):
    del init_ref

    def start(r, carry):
        pltpu.make_async_copy(x_ref.at[r], o_ref.at[d1_ref[0, 0, r]], sem).start()
        pltpu.make_async_copy(x_ref.at[r], o_ref.at[d2_ref[0, 0, r]], sem).start()
        return carry

    lax.fori_loop(0, tm, start, 0)

    def wait(r, carry):
        pltpu.make_async_copy(x_ref.at[r], o_ref.at[d1_ref[0, 0, r]], sem).wait()
        pltpu.make_async_copy(x_ref.at[r], o_ref.at[d2_ref[0, 0, r]], sem).wait()
        return carry

    lax.fori_loop(0, tm, wait, 0)


def moe_scatter(x3, d1, d2, rows_out, *, tm):
    rows = x3.shape[0]
    assert rows % tm == 0
    nsteps = rows // tm
    init = jnp.zeros((rows_out,) + ROW_SLAB, x3.dtype)
    smem = lambda: pl.BlockSpec((1, 1, tm), lambda i: (i, 0, 0), memory_space=pltpu.SMEM)
    return pl.pallas_call(
        functools.partial(_scatter_kernel, tm=tm),
        grid=(nsteps,),
        in_specs=[smem(), smem(),
                  pl.BlockSpec((tm,) + ROW_SLAB, lambda i: (i, 0, 0)),
                  pl.BlockSpec(memory_space=pl.ANY)],
        out_specs=pl.BlockSpec(memory_space=pl.ANY),
        out_shape=jax.ShapeDtypeStruct((rows_out,) + ROW_SLAB, x3.dtype),
        scratch_shapes=[pltpu.SemaphoreType.DMA(())],
        input_output_aliases={3: 0},
        compiler_params=_cparams(("arbitrary",)),
    )(d1.reshape(nsteps, 1, tm), d2.reshape(nsteps, 1, tm), x3, init)


def _gather_kernel(d1_ref, d2_ref, y_ref, o1_ref, o2_ref, sem, *, tm):
    def start(r, carry):
        pltpu.make_async_copy(y_ref.at[d1_ref[0, 0, r]], o1_ref.at[r], sem).start()
        pltpu.make_async_copy(y_ref.at[d2_ref[0, 0, r]], o2_ref.at[r], sem).start()
        return carry

    lax.fori_loop(0, tm, start, 0)

    def wait(r, carry):
        pltpu.make_async_copy(y_ref.at[d1_ref[0, 0, r]], o1_ref.at[r], sem).wait()
        pltpu.make_async_copy(y_ref.at[d2_ref[0, 0, r]], o2_ref.at[r], sem).wait()
        return carry

    lax.fori_loop(0, tm, wait, 0)


def moe_gather(y3, d1, d2, *, tm):
    rows = d1.shape[0]
    assert rows % tm == 0
    nsteps = rows // tm
    smem = lambda: pl.BlockSpec((1, 1, tm), lambda i: (i, 0, 0), memory_space=pltpu.SMEM)
    blk = lambda: pl.BlockSpec((tm,) + ROW_SLAB, lambda i: (i, 0, 0))
    shp = jax.ShapeDtypeStruct((rows,) + ROW_SLAB, y3.dtype)
    return pl.pallas_call(
        functools.partial(_gather_kernel, tm=tm),
        grid=(nsteps,),
        in_specs=[smem(), smem(), pl.BlockSpec(memory_space=pl.ANY)],
        out_specs=[blk(), blk()],
        out_shape=[shp, shp],
        scratch_shapes=[pltpu.SemaphoreType.DMA(())],
        compiler_params=_cparams(("arbitrary",)),
    )(d1.reshape(nsteps, 1, tm), d2.reshape(nsteps, 1, tm), y3)


def _combine_ln_kernel(x_ref, y1_ref, y2_ref, meta_ref, g_ref, b_ref, of_ref, ob_ref):
    meta = meta_ref[...]
    g1 = meta[:, META_G1:META_G1 + 1]
    g2 = meta[:, META_G2:META_G2 + 1]
    y = ALPHA * x_ref[...] + (g1 * y1_ref[...] + g2 * y2_ref[...])
    out = _layer_norm_rows(y, g_ref[...], b_ref[...])
    of_ref[...] = out
    ob_ref[...] = out.astype(BF16)


def moe_combine_ln(x, y1, y2, meta, g, b, *, tm):
    rows = x.shape[0]
    assert rows % tm == 0
    row = lambda a: a.reshape(1, D_MODEL).astype(F32)
    blk = lambda w: pl.BlockSpec((tm, w), lambda i: (i, 0))
    vec = pl.BlockSpec((1, D_MODEL), lambda i: (0, 0))
    return pl.pallas_call(
        _combine_ln_kernel,
        grid=(rows // tm,),
        in_specs=[blk(D_MODEL), blk(D_MODEL), blk(D_MODEL), blk(LANES), vec, vec],
        out_specs=[blk(D_MODEL), blk(D_MODEL)],
        out_shape=[jax.ShapeDtypeStruct((rows, D_MODEL), F32), jax.ShapeDtypeStruct((rows, D_MODEL), BF16)],
        compiler_params=_cparams(("arbitrary",)),
    )(x, y1, y2, meta, row(g), row(b))


MOE_TM = 512


def moe_layer(xf, w_router, w_gu, w_down, ln_g, ln_b, *, tm_tok):
    tokens = xf.shape[0]
    meta, counts = moe_router(xf, w_router, tm=ROUTER_TM)
    counts = counts[0, :N_EXPERTS].astype(jnp.int32)
    padded = ((counts + MOE_TM - 1) // MOE_TM) * MOE_TM
    ends = jnp.cumsum(padded)
    starts = ends - padded
    n_tiles = (2 * tokens + N_EXPERTS * (MOE_TM - 1)) // MOE_TM
    rows_out = n_tiles * MOE_TM
    e1 = meta[:, META_E1].astype(jnp.int32)
    e2 = meta[:, META_E2].astype(jnp.int32)
    d1 = starts[e1] + meta[:, META_R1].astype(jnp.int32)
    d2 = starts[e2] + meta[:, META_R2].astype(jnp.int32)
    tile_start = jnp.arange(n_tiles, dtype=jnp.int32) * MOE_TM
    tile_expert = jnp.minimum(jnp.sum((tile_start[:, None] >= ends[None, :]).astype(jnp.int32), axis=1),
                              N_EXPERTS - 1).astype(jnp.int32)
    n_used = (ends[N_EXPERTS - 1] // MOE_TM).astype(jnp.int32).reshape(1)
    xs3 = moe_scatter(xf.reshape((tokens,) + ROW_SLAB), d1, d2, rows_out, tm=tm_tok)
    xs = xs3.reshape(rows_out, D_MODEL)
    (hmid,) = ws_matmul(xs, w_gu, n_out=EXPERT_DIM, tm=MOE_TM, tn=1024, act="swiglu",
                        tile_expert=tile_expert, n_used=n_used, out_dtypes=(BF16,))
    (ys,) = ws_matmul(hmid, w_down, n_out=D_MODEL, tm=MOE_TM, tn=512, act="none",
                      tile_expert=tile_expert, n_used=n_used, out_dtypes=(F32,))
    y1, y2 = moe_gather(ys.reshape((rows_out,) + ROW_SLAB), d1, d2, tm=tm_tok)
    return moe_combine_ln(xf, y1.reshape(tokens, D_MODEL), y2.reshape(tokens, D_MODEL), meta, ln_g, ln_b,
                          tm=tm_tok)


TOK_TM = 1376
LN_TM = 688
LN_TK = 512
ROUTER_TM = 384


def kernel(x_prompt, x_sample, state_cconv, cache_sb_k, cache_sb_v, state_ssm_conv, state_ssm, cc_w_pw1, cc_b_pw1, cc_w_dw, cc_b_dw, cc_ln_g, cc_ln_b, cc_w_pw2, cc_b_pw2, sb_w_qkv, sb_w_o, ssm_w_in, ssm_conv_w, ssm_conv_b, ssm_dt_bias, ssm_A_log, ssm_D, ssm_norm_w, ssm_w_out, ffn_w_gu, ffn_w_down, moe_w_router, moe_w_gu, moe_w_down, ln_g, ln_b):
    bp, seq, d = x_prompt.shape
    bs, lq, _ = x_sample.shape
    n_p = bp * seq
    n_s = bs * lq
    tokens = n_p + n_s
    assert tokens % TOK_TM == 0 and tokens % LN_TM == 0
    xf = jnp.concatenate([x_prompt.reshape(n_p, d), x_sample.reshape(n_s, d)], axis=0)
    xb = xf.astype(BF16)
    zeros_d = jnp.zeros((d,), F32)
    tri = (jnp.arange(SB_SUB)[:, None] > jnp.arange(SB_SUB)[None, :]).astype(BF16)

    new_cc_p, new_cc_s = [], []
    for i in range(DEPTH):
        kind = i % 3
        j = i // 3
        if kind == 0:
            halo = 32
            (u,) = ws_matmul(xb, cc_w_pw1[j].astype(BF16)[None], n_out=d, tm=TOK_TM, tn=512, act="glu",
                             bias=cc_b_pw1[j].reshape(1, 1, 2 * d), out_dtypes=(F32,))
            u_p = u[:n_p].reshape(bp, seq, d)
            u_s = u[n_p:].reshape(bs, lq, d)
            hist_p = jnp.zeros((bp, halo, d), F32)
            hist_s = jnp.pad(state_cconv[j], ((0, 0), (halo - (CC_WIDTH - 1), 0), (0, 0)))
            conv = functools.partial(dwconv, w=cc_w_dw[j], cbias=cc_b_dw[j], ln_g=cc_ln_g[j], ln_b=cc_ln_b[j],
                                     taps=CC_WIDTH, halo=halo, tc=d, mode="ln_silu", out_dtype=BF16)
            c_p = conv(u_p, hist_p, tl=128)
            c_s = conv(u_s, hist_s, tl=lq)
            c = jnp.concatenate([c_p.reshape(n_p, d), c_s.reshape(n_s, d)], axis=0)
            new_cc_p.append(u_p[:, seq - (CC_WIDTH - 1):])
            new_cc_s.append(jnp.concatenate([state_cconv[j], u_s], axis=1)[:, lq:])
            xf, xb = xs_matmul_ln(c, cc_w_pw2[j].astype(BF16), cc_b_pw2[j], xf, ln_g[i, 0], ln_b[i, 0],
                                  tm=LN_TM, tk=LN_TK)
        elif kind == 1:
            qkv_f, qkv_b = ws_matmul(xb, sb_w_qkv[j].astype(BF16)[None], n_out=3 * d, tm=TOK_TM, tn=1024,
                                     out_dtypes=(F32, BF16))
            k_p = qkv_f[:n_p, d:2 * d].reshape(1, bp, seq, SB_HEADS, SB_HEAD_DIM)
            v_p = qkv_f[:n_p, 2 * d:].reshape(1, bp, seq, SB_HEADS, SB_HEAD_DIM)
            k_s = qkv_f[n_p:, d:2 * d].reshape(1, bs, lq, SB_HEADS, SB_HEAD_DIM)
            v_s = qkv_f[n_p:, 2 * d:].reshape(1, bs, lq, SB_HEADS, SB_HEAD_DIM)
            o_p = sb_attention_prompt(qkv_b, tri, nbatch=bp, seq=seq, tq=512, tk=512)
            past = cache_sb_k.shape[2]
            o_s = sb_attention_sample(qkv_b, cache_sb_k[j].reshape(bs, past, d), cache_sb_v[j].reshape(bs, past, d),
                                      tri, row0=n_p, nbatch=bs, lq=lq, tk=512)
            o = jnp.concatenate([o_p, o_s], axis=0)
            xf, xb = xs_matmul_ln(o, sb_w_o[j].astype(BF16), zeros_d, xf, ln_g[i, 0], ln_b[i, 0],
                                  tm=LN_TM, tk=LN_TK)
        else:
            w_in = ssm_w_in[j]
            nzx = SSM_D_INNER + SSM_CONV_DIM
            (zx,) = ws_matmul(xb, w_in[:, :nzx].astype(BF16)[None], n_out=nzx, tm=TOK_TM, tn=1024,
                              out_dtypes=(F32,))
            w_dt = jnp.zeros((d, LANES), BF16).at[:, :SSM_HEADS].set(w_in[:, nzx:].astype(BF16))
            (dt_raw,) = ws_matmul(xb, w_dt[None], n_out=LANES, tm=TOK_TM, tn=LANES, out_dtypes=(F32,))
            xbc_p = zx[:n_p, SSM_D_INNER:].reshape(bp, seq, SSM_CONV_DIM)
            xbc_s = zx[n_p:, SSM_D_INNER:].reshape(bs, lq, SSM_CONV_DIM)
            halo = 8
            hist_p = jnp.zeros((bp, halo, SSM_CONV_DIM), F32)
            hist_s = jnp.pad(state_ssm_conv[j], ((0, 0), (halo - (SSM_CONV - 1), 0), (0, 0)))
            conv = functools.partial(dwconv, w=ssm_conv_w[j], cbias=ssm_conv_b[j], ln_g=None, ln_b=None,
                                     taps=SSM_CONV, halo=halo, tc=2048, mode="silu", out_dtype=F32)
            a_p = conv(xbc_p, hist_p, tl=256)
            a_s = conv(xbc_s, hist_s, tl=lq)
            pad_s = SSD_Q - lq
            a_s = jnp.pad(a_s, ((0, 0), (0, pad_s), (0, 0)))
            z_p = zx[:n_p, :SSM_D_INNER].reshape(bp, seq, SSM_D_INNER)
            z_s = jnp.pad(zx[n_p:, :SSM_D_INNER].reshape(bs, lq, SSM_D_INNER), ((0, 0), (0, pad_s), (0, 0)))
            dt_p = dt_raw[:n_p].reshape(bp, seq, SSM_HPAD)
            dt_s = jnp.pad(dt_raw[n_p:].reshape(bs, lq, SSM_HPAD), ((0, 0), (0, pad_s), (0, 0)))
            st0_p = jnp.zeros((bp, SSM_D_STATE, SSM_D_INNER), F32)
            st0_s = jnp.swapaxes(state_ssm[j].reshape(bs, SSM_D_INNER, SSM_D_STATE), 1, 2)
            scan = functools.partial(ssd_scan, dt_bias=ssm_dt_bias[j], a_log=ssm_A_log[j], d_skip=ssm_D[j],
                                     norm_w=ssm_norm_w[j])
            y_p, st_p = scan(a_p, z_p, dt_p, st0_p, valid=seq)
            y_s, st_s = scan(a_s, z_s, dt_s, st0_s, valid=lq)
            y = jnp.concatenate([y_p.reshape(n_p, SSM_D_INNER), y_s[:, :lq].reshape(n_s, SSM_D_INNER)], axis=0)
            new_sc_p = xbc_p[:, seq - (SSM_CONV - 1):][None]
            new_sc_s = jnp.concatenate([state_ssm_conv[j], xbc_s], axis=1)[:, lq:][None]
            unt = lambda s, n: jnp.swapaxes(s, 1, 2).reshape(1, n, SSM_HEADS, SSM_HEAD_DIM, SSM_D_STATE)
            new_ss_p = unt(st_p, bp)
            new_ss_s = unt(st_s, bs)
            xf, xb = xs_matmul_ln(y, ssm_w_out[j].astype(BF16), zeros_d, xf, ln_g[i, 0], ln_b[i, 0],
                                  tm=LN_TM, tk=LN_TK)
        f = i // 2
        if i % 2 == 0:
            (hmid,) = ws_matmul(xb, ffn_w_gu[f].astype(BF16)[None], n_out=FFN_DIM, tm=TOK_TM, tn=512,
                                act="swiglu", out_dtypes=(BF16,))
            xf, xb = xs_matmul_ln(hmid, ffn_w_down[f].astype(BF16), zeros_d, xf, ln_g[i, 1], ln_b[i, 1],
                                  tm=LN_TM, tk=LN_TK)
        else:
            xf, xb = moe_layer(xf, moe_w_router[f], moe_w_gu[f].astype(BF16), moe_w_down[f].astype(BF16),
                               ln_g[i, 1], ln_b[i, 1], tm_tok=LN_TM)

    y_prompt = xf[:n_p].reshape(bp, seq, d)
    y_sample = xf[n_p:].reshape(bs, lq, d)
    return (y_prompt, y_sample, jnp.stack(new_cc_p), jnp.stack(new_cc_s), k_p, v_p, k_s, v_s,
            new_sc_p, new_sc_s, new_ss_p, new_ss_s)
```

```python
import functools
import math

import jax
import jax.numpy as jnp
from jax import lax
from jax.experimental import pallas as pl
from jax.experimental.pallas import tpu as pltpu

F32 = jnp.float32
BF16 = jnp.bfloat16

D_MODEL = 2048
DEPTH = 4
ALPHA = (2.0 * DEPTH) ** 0.25
LN_EPS = 1e-5
CC_WIDTH = 31
SB_HEADS = 16
SB_HEAD_DIM = 128
SSM_D_INNER = 4096
SSM_HEAD_DIM = 64
SSM_HEADS = 64
SSM_GROUPS = 8
SSM_D_STATE = 128
SSM_CONV = 4
SSM_CONV_DIM = SSM_D_INNER + 2 * SSM_GROUPS * SSM_D_STATE
FFN_DIM = 5632
N_EXPERTS = 8
EXPERT_DIM = 7168

VMEM_LIMIT_BYTES = 56 * 1024 * 1024
LANES = 128
SUBLANES = 8


def _cparams(sem):
    return pltpu.CompilerParams(dimension_semantics=sem, vmem_limit_bytes=VMEM_LIMIT_BYTES)


def _bdot(a, b):
    return jnp.dot(a.astype(BF16), b.astype(BF16), preferred_element_type=F32)


def _split_dot(a, onehot_bf16, passes):
    out = None
    rem = a
    for p in range(passes):
        piece = rem.astype(BF16)
        term = jnp.dot(piece, onehot_bf16, preferred_element_type=F32)
        out = term if out is None else out + term
        if p + 1 < passes:
            rem = rem - piece.astype(F32)
    return out


def _split_dot_left(onehot_bf16, a, passes):
    out = None
    rem = a
    for p in range(passes):
        piece = rem.astype(BF16)
        term = jnp.dot(onehot_bf16, piece, preferred_element_type=F32)
        out = term if out is None else out + term
        if p + 1 < passes:
            rem = rem - piece.astype(F32)
    return out


def _layer_norm_rows(y, g, b):
    mu = jnp.mean(y, axis=-1, keepdims=True)
    yc = y - mu
    var = jnp.mean(yc * yc, axis=-1, keepdims=True)
    return yc * lax.rsqrt(var + LN_EPS) * g + b


def _sigmoid(x):
    return 1.0 / (1.0 + jnp.exp(-x))


def _silu(x):
    return x * _sigmoid(x)


def _softplus(x):
    return jnp.maximum(x, 0.0) + jnp.log(1.0 + jnp.exp(-jnp.abs(x)))


def _ws_kernel(te_ref, nu_ref, *refs, act, has_bias, n_outs):
    del te_ref
    i = pl.program_id(1)
    outs = refs[len(refs) - n_outs:]
    ins = refs[:len(refs) - n_outs]
    x_ref = ins[0]
    two = act in ("glu", "swiglu")

    @pl.when(i < nu_ref[0])
    def _():
        x = x_ref[...].astype(BF16)
        a = jnp.dot(x, ins[1][0], preferred_element_type=F32)
        if two:
            b = jnp.dot(x, ins[2][0], preferred_element_type=F32)
        if has_bias:
            a = a + ins[3 if two else 2][0]
            if two:
                b = b + ins[4][0]
        if act == "glu":
            r = a * _sigmoid(b)
        elif act == "swiglu":
            r = _silu(a) * b
        else:
            r = a
        for o in outs:
            o[...] = r.astype(o.dtype)

    @pl.when(i >= nu_ref[0])
    def _():
        for o in outs:
            o[...] = jnp.zeros(o.shape, o.dtype)


def ws_matmul(x, w, *, n_out, tm, tn, act="none", bias=None, tile_expert=None, n_used=None,
              out_dtypes=(F32,)):
    rows, kdim = x.shape
    assert rows % tm == 0 and n_out % tn == 0
    n_row_tiles = rows // tm
    n_col_tiles = n_out // tn
    two = act in ("glu", "swiglu")
    if tile_expert is None:
        tile_expert = jnp.zeros((n_row_tiles,), jnp.int32)
        n_used = jnp.full((1,), n_row_tiles, jnp.int32)
    in_specs = [pl.BlockSpec((tm, kdim), lambda j, i, te, nu: (i, 0)),
                pl.BlockSpec((1, kdim, tn), lambda j, i, te, nu: (te[i], 0, j))]
    args = [x, w]
    if two:
        in_specs.append(pl.BlockSpec((1, kdim, tn), lambda j, i, te, nu: (te[i], 0, j + n_col_tiles)))
        args.append(w)
    if bias is not None:
        in_specs.append(pl.BlockSpec((1, 1, tn), lambda j, i, te, nu: (te[i], 0, j)))
        args.append(bias)
        if two:
            in_specs.append(pl.BlockSpec((1, 1, tn), lambda j, i, te, nu: (te[i], 0, j + n_col_tiles)))
            args.append(bias)
    out_specs = [pl.BlockSpec((tm, tn), lambda j, i, te, nu: (i, j)) for _ in out_dtypes]
    out_shape = [jax.ShapeDtypeStruct((rows, n_out), dt) for dt in out_dtypes]
    res = pl.pallas_call(
        functools.partial(_ws_kernel, act=act, has_bias=bias is not None, n_outs=len(out_dtypes)),
        grid_spec=pltpu.PrefetchScalarGridSpec(
            num_scalar_prefetch=2, grid=(n_col_tiles, n_row_tiles),
            in_specs=in_specs, out_specs=out_specs),
        out_shape=out_shape,
        compiler_params=_cparams(("arbitrary", "arbitrary")),
    )(tile_expert, n_used, *args)
    return res


def _xs_ln_kernel(x_ref, w_ref, bias_ref, res_ref, g_ref, b_ref, of_ref, ob_ref, acc_ref, *, nk):
    k = pl.program_id(1)

    @pl.when(k == 0)
    def _():
        acc_ref[...] = jnp.zeros(acc_ref.shape, F32)

    acc_ref[...] += jnp.dot(x_ref[...].astype(BF16), w_ref[...], preferred_element_type=F32)

    @pl.when(k == nk - 1)
    def _():
        y = ALPHA * res_ref[...] + (acc_ref[...] + bias_ref[...])
        out = _layer_norm_rows(y, g_ref[...], b_ref[...])
        of_ref[...] = out
        ob_ref[...] = out.astype(BF16)


def xs_matmul_ln(x, w, bias, res, g, b, *, tm, tk):
    rows, kdim = x.shape
    n = w.shape[1]
    assert rows % tm == 0 and kdim % tk == 0 and n == D_MODEL
    nk = kdim // tk
    row = lambda a: a.reshape(1, n).astype(F32)
    return pl.pallas_call(
        functools.partial(_xs_ln_kernel, nk=nk),
        grid=(rows // tm, nk),
        in_specs=[pl.BlockSpec((tm, tk), lambda i, k: (i, k)),
                  pl.BlockSpec((tk, n), lambda i, k: (k, 0)),
                  pl.BlockSpec((1, n), lambda i, k: (0, 0)),
                  pl.BlockSpec((tm, n), lambda i, k: (i, 0)),
                  pl.BlockSpec((1, n), lambda i, k: (0, 0)),
                  pl.BlockSpec((1, n), lambda i, k: (0, 0))],
        out_specs=[pl.BlockSpec((tm, n), lambda i, k: (i, 0)),
                   pl.BlockSpec((tm, n), lambda i, k: (i, 0))],
        out_shape=[jax.ShapeDtypeStruct((rows, n), F32), jax.ShapeDtypeStruct((rows, n), BF16)],
        scratch_shapes=[pltpu.VMEM((tm, n), F32)],
        compiler_params=_cparams(("arbitrary", "arbitrary")),
    )(x, w, row(bias), res, row(g), row(b))


CONV_ROW_CHUNK = 32
CONV_LANE_CHUNK = 512


def _dwconv_kernel(hist_ref, prev_ref, cur_ref, w_ref, cb_ref, g_ref, b_ref, o_ref, win_ref, y_ref, sh_ref,
                   *, taps, halo, tl, mode, single_step, preshift):
    i = pl.program_id(1)

    @pl.when(i == 0)
    def _():
        win_ref[0:halo, :] = hist_ref[0]

    if not single_step:
        @pl.when(i > 0)
        def _():
            win_ref[0:halo, :] = prev_ref[...]

    win_ref[halo:halo + tl, :] = cur_ref[...]
    off = halo - (taps - 1)
    width = cur_ref.shape[1]
    rc = min(CONV_ROW_CHUNK, tl)
    lc = min(CONV_LANE_CHUNK, width)
    span = halo + tl - SUBLANES
    if preshift:
        for r in range(1, SUBLANES):
            for c0 in range(0, width, lc):
                sh_ref[r - 1, :, c0:c0 + lc] = win_ref[r:r + span, c0:c0 + lc]

    def tap_rows(k, r0, c0):
        s = off + k
        r = s % SUBLANES
        if not preshift or r == 0:
            return win_ref[s + r0:s + r0 + rc, c0:c0 + lc]
        return sh_ref[r - 1, s - r + r0:s - r + r0 + rc, c0:c0 + lc]

    for r0 in range(0, tl, rc):
        for c0 in range(0, width, lc):
            acc = jnp.zeros((rc, lc), F32) + cb_ref[:, c0:c0 + lc]
            for k in range(taps):
                acc = acc + w_ref[k:k + 1, c0:c0 + lc] * tap_rows(k, r0, c0)
            if mode == "silu":
                o_ref[0, r0:r0 + rc, c0:c0 + lc] = _silu(acc).astype(o_ref.dtype)
            else:
                y_ref[r0:r0 + rc, c0:c0 + lc] = acc
    if mode == "ln_silu":
        y = _layer_norm_rows(y_ref[...], g_ref[...], b_ref[...])
        o_ref[0] = _silu(y).astype(o_ref.dtype)


def dwconv(x, hist, w, cbias, ln_g, ln_b, *, row0, nseq, length, col0, taps, halo, tl, tc, mode, out_dtype):
    chans = hist.shape[2]
    assert length % tl == 0 and chans % tc == 0 and row0 % tl == 0 and col0 % tc == 0
    single_step = length == tl
    cur0 = row0 // tl
    cb0 = col0 // tc
    steps = length // tl
    if single_step:
        prev_arr = hist
        prev_spec = pl.BlockSpec((1, halo, tc), lambda s, i, c: (s, 0, c))
    else:
        assert tl % halo == 0 and row0 % halo == 0
        per = tl // halo
        prev_arr = x
        prev_spec = pl.BlockSpec(
            (halo, tc), lambda s, i, c: (jnp.maximum(row0 // halo + (s * steps + i) * per - 1, 0), cb0 + c))
    row = lambda a: a.reshape(1, chans).astype(F32)
    preshift = taps > SUBLANES
    shifted = (SUBLANES - 1, halo + tl - SUBLANES, tc) if preshift else (1, SUBLANES, LANES)
    if ln_g is None:
        ln_g = jnp.ones((chans,), F32)
        ln_b = jnp.zeros((chans,), F32)
    return pl.pallas_call(
        functools.partial(_dwconv_kernel, taps=taps, halo=halo, tl=tl, mode=mode, single_step=single_step,
                          preshift=preshift),
        grid=(nseq, steps, chans // tc),
        in_specs=[pl.BlockSpec((1, halo, tc), lambda s, i, c: (s, 0, c)),
                  prev_spec,
                  pl.BlockSpec((tl, tc), lambda s, i, c: (cur0 + s * steps + i, cb0 + c)),
                  pl.BlockSpec((taps, tc), lambda s, i, c: (0, c)),
                  pl.BlockSpec((1, tc), lambda s, i, c: (0, c)),
                  pl.BlockSpec((1, tc), lambda s, i, c: (0, c)),
                  pl.BlockSpec((1, tc), lambda s, i, c: (0, c))],
        out_specs=pl.BlockSpec((1, tl, tc), lambda s, i, c: (s, i, c)),
        out_shape=jax.ShapeDtypeStruct((nseq, length, chans), out_dtype),
        scratch_shapes=[pltpu.VMEM((halo + tl, tc), F32), pltpu.VMEM((tl, tc), F32), pltpu.VMEM(shifted, F32)],
        compiler_params=_cparams(("arbitrary", "arbitrary", "arbitrary")),
    )(hist, prev_arr, x, w.astype(F32), row(cbias), row(ln_g), row(ln_b))


SB_SUB = 256


def _sb_block(q, kb, vb, tri, carry, mask):
    log_beta, suffix, total = _sb_scores(q, kb, tri, mask)
    contrib = _sb_weighted(log_beta, suffix, carry, vb, mask)
    return contrib, carry + total


def _sb_scores(q, kb, tri, mask):
    z = lax.dot_general(q, kb, (((1,), (1,)), ((), ())), preferred_element_type=F32)
    z = z * (SB_HEAD_DIM ** -0.5)
    sp = _softplus(z)
    log_keep = -sp
    if mask is not None:
        log_keep = jnp.where(mask, log_keep, 0.0)
    suffix = jnp.dot(log_keep.astype(BF16), tri, preferred_element_type=F32)
    return z - sp, suffix, suffix[:, 0:1] + log_keep[:, 0:1]


def _sb_weighted(log_beta, suffix, carry, vb, mask):
    w = jnp.exp(log_beta + suffix + carry)
    if mask is not None:
        w = jnp.where(mask, w, 0.0)
    return jnp.dot(w.astype(BF16), vb, preferred_element_type=F32)


def _sb_prompt_kernel(qi_ref, kj_ref, q_ref, k_ref, v_ref, tri_ref, o_ref, acc_ref, carry_ref, *, tq, tk):
    p = pl.program_id(2)
    qi = qi_ref[p]
    kj = kj_ref[p]

    @pl.when(kj == qi)
    def _():
        acc_ref[...] = jnp.zeros(acc_ref.shape, F32)
        carry_ref[...] = jnp.zeros(carry_ref.shape, F32)

    def step(masked):
        q = q_ref[...]
        tri = tri_ref[...]
        nsub = tk // SB_SUB
        stats = []
        for s in range(nsub):
            mask = None
            if masked:
                q_pos = lax.broadcasted_iota(jnp.int32, (tq, SB_SUB), 0)
                k_pos = s * SB_SUB + lax.broadcasted_iota(jnp.int32, (tq, SB_SUB), 1)
                mask = k_pos < q_pos
            stats.append(_sb_scores(q, k_ref[s * SB_SUB:(s + 1) * SB_SUB, :], tri, mask) + (mask,))
        acc = acc_ref[...]
        carry = carry_ref[...]
        for s in range(nsub - 1, -1, -1):
            log_beta, suffix, total, mask = stats[s]
            acc = acc + _sb_weighted(log_beta, suffix, carry, v_ref[s * SB_SUB:(s + 1) * SB_SUB, :], mask)
            carry = carry + total
        acc_ref[...] = acc
        carry_ref[...] = carry

    @pl.when(kj == qi)
    def _():
        step(True)

    @pl.when(kj != qi)
    def _():
        step(False)

    @pl.when(kj == 0)
    def _():
        o_ref[...] = acc_ref[...].astype(o_ref.dtype)


def sb_attention_prompt(qkv, tri, *, nbatch, seq, tq, tk):
    assert tq == tk and seq % tq == 0 and tk % SB_SUB == 0
    nq = seq // tq
    pairs = [(a, b) for a in range(nq) for b in range(a, -1, -1)]
    qi = jnp.asarray([a for a, _ in pairs], jnp.int32)
    kj = jnp.asarray([b for _, b in pairs], jnp.int32)
    h = SB_HEADS
    return pl.pallas_call(
        functools.partial(_sb_prompt_kernel, tq=tq, tk=tk),
        grid_spec=pltpu.PrefetchScalarGridSpec(
            num_scalar_prefetch=2, grid=(nbatch, h, len(pairs)),
            in_specs=[pl.BlockSpec((tq, SB_HEAD_DIM), lambda b, hh, p, qi, kj: (b * nq + qi[p], hh)),
                      pl.BlockSpec((tk, SB_HEAD_DIM), lambda b, hh, p, qi, kj: (b * nq + kj[p], h + hh)),
                      pl.BlockSpec((tk, SB_HEAD_DIM), lambda b, hh, p, qi, kj: (b * nq + kj[p], 2 * h + hh)),
                      pl.BlockSpec((SB_SUB, SB_SUB), lambda b, hh, p, qi, kj: (0, 0))],
            out_specs=pl.BlockSpec((tq, SB_HEAD_DIM), lambda b, hh, p, qi, kj: (b * nq + qi[p], hh)),
            scratch_shapes=[pltpu.VMEM((tq, SB_HEAD_DIM), F32), pltpu.VMEM((tq, 1), F32)]),
        out_shape=jax.ShapeDtypeStruct((nbatch * seq, D_MODEL), BF16),
        compiler_params=_cparams(("arbitrary", "arbitrary", "arbitrary")),
    )(qi, kj, qkv, qkv, qkv, tri)


def _sb_sample_kernel(q_ref, kn_ref, vn_ref, kp_ref, vp_ref, tri_ref, o_ref, acc_ref, carry_ref,
                      *, lq, tk, nsteps):
    b = pl.program_id(0)
    s = pl.program_id(2)
    q = q_ref[...]
    tri = tri_ref[...]

    @pl.when(s == 0)
    def _():
        n = kn_ref.shape[0]
        row = lax.broadcasted_iota(jnp.int32, (lq, n), 0)
        col = lax.broadcasted_iota(jnp.int32, (lq, n), 1)
        mask = (col >= b * lq) & (col < b * lq + row)
        contrib, carry = _sb_block(q, kn_ref[...], vn_ref[...], tri[0:n, 0:n], jnp.zeros((lq, 1), F32), mask)
        acc_ref[...] = contrib
        carry_ref[...] = carry

    @pl.when(s > 0)
    def _():
        acc = acc_ref[...]
        carry = carry_ref[...]
        for u in range(tk // SB_SUB - 1, -1, -1):
            kb = kp_ref[0, u * SB_SUB:(u + 1) * SB_SUB, :].astype(BF16)
            vb = vp_ref[0, u * SB_SUB:(u + 1) * SB_SUB, :].astype(BF16)
            contrib, carry = _sb_block(q, kb, vb, tri, carry, None)
            acc = acc + contrib
        acc_ref[...] = acc
        carry_ref[...] = carry

    @pl.when(s == nsteps - 1)
    def _():
        o_ref[...] = acc_ref[...].astype(o_ref.dtype)


def sb_attention_sample(qkv, past_k, past_v, tri, *, row0, nbatch, lq, tk):
    past = past_k.shape[1]
    nnew = nbatch * lq
    assert past % tk == 0 and row0 % nnew == 0 and row0 % lq == 0 and nnew <= SB_SUB
    nsteps = 1 + past // tk
    h = SB_HEADS
    nb = past // tk
    return pl.pallas_call(
        functools.partial(_sb_sample_kernel, lq=lq, tk=tk, nsteps=nsteps),
        grid=(nbatch, h, nsteps),
        in_specs=[pl.BlockSpec((lq, SB_HEAD_DIM), lambda b, hh, s: (row0 // lq + b, hh)),
                  pl.BlockSpec((nnew, SB_HEAD_DIM), lambda b, hh, s: (row0 // nnew, h + hh)),
                  pl.BlockSpec((nnew, SB_HEAD_DIM), lambda b, hh, s: (row0 // nnew, 2 * h + hh)),
                  pl.BlockSpec((1, tk, SB_HEAD_DIM), lambda b, hh, s: (b, jnp.minimum(nb - s, nb - 1), hh)),
                  pl.BlockSpec((1, tk, SB_HEAD_DIM), lambda b, hh, s: (b, jnp.minimum(nb - s, nb - 1), hh)),
                  pl.BlockSpec((SB_SUB, SB_SUB), lambda b, hh, s: (0, 0))],
        out_specs=pl.BlockSpec((lq, SB_HEAD_DIM), lambda b, hh, s: (b, hh)),
        out_shape=jax.ShapeDtypeStruct((nnew, D_MODEL), BF16),
        scratch_shapes=[pltpu.VMEM((lq, SB_HEAD_DIM), F32), pltpu.VMEM((lq, 1), F32)],
        compiler_params=_cparams(("arbitrary", "arbitrary", "arbitrary")),
    )(qkv, qkv, qkv, past_k, past_v, tri)


SSD_Q = 128
SSM_HPAD = 128


def _ssd_kernel(xbc_ref, z_ref, dt_ref, dtt_ref, st0_ref, dtb_ref, dtbc_ref, a_ref, ac_ref, dx_ref, nw_ref,
                e64_ref, eq_ref, lincl_ref, uincl_ref, y_ref, stout_ref, st_ref, yscr_ref, *, valid, nchunks):
    c = pl.program_id(1)
    q = SSD_Q
    hd = SSM_HEAD_DIM
    pair = 2 * hd

    @pl.when(c == 0)
    def _():
        st_ref[...] = st0_ref[0]

    row_t = lax.broadcasted_iota(jnp.int32, (q, SSM_HPAD), 0) + c * q
    dt = jnp.where(row_t < valid, _softplus(dt_ref[0] + dtb_ref[...]), 0.0)
    col_t = lax.broadcasted_iota(jnp.int32, (SSM_HPAD, q), 1) + c * q
    dtt = jnp.where(col_t < valid, _softplus(dtt_ref[0] + dtbc_ref[...]), 0.0)
    cum = _split_dot_left(lincl_ref[...], dt * a_ref[...], 3)
    cumt = _split_dot(dtt * ac_ref[...], uincl_ref[...], 3)
    e64 = e64_ref[...]
    dtx = _split_dot(dt, e64, 2)
    cumx = _split_dot(cum, e64, 3)
    colc = _split_dot(cum, eq_ref[...], 3)
    clx = cumx[q - 1:q, :]
    xs = xbc_ref[0, :, 0:SSM_D_INNER]
    xdt = xs * dtx
    xt = xdt.astype(BF16)
    xw = (xdt * jnp.exp(clx - cumx)).astype(BF16)
    ecum = jnp.exp(cumx)
    sdec = jnp.exp(clx)
    causal = lax.broadcasted_iota(jnp.int32, (q, q), 1) <= lax.broadcasted_iota(jnp.int32, (q, q), 0)
    lane_lo = lax.broadcasted_iota(jnp.int32, (q, pair), 1) < hd
    gn = SSM_GROUPS * SSM_D_STATE
    hpg = SSM_HEADS // SSM_GROUPS
    for g in range(SSM_GROUPS):
        bg_f = xbc_ref[0, :, SSM_D_INNER + g * SSM_D_STATE:SSM_D_INNER + (g + 1) * SSM_D_STATE]
        bg = bg_f.astype(BF16)
        bgt = bg_f.T.astype(BF16)
        cg =xbc_ref[0, :, SSM_D_INNER + gn + g * SSM_D_STATE:SSM_D_INNER + gn + (g + 1) * SSM_D_STATE].astype(BF16)
        cb = lax.dot_general(cg, bg, (((1,), (1,)), ((), ())), preferred_element_type=F32)
        for j in range(g * hpg // 2, (g + 1) * hpg // 2):
            lanes = slice(j * pair, (j + 1) * pair)
            ys = []
            for hh in range(2):
                h = 2 * j + hh
                seg = colc[:, h * q:(h + 1) * q] - cumt[h:h + 1, :]
                m = jnp.where(causal, cb * jnp.exp(seg), 0.0).astype(BF16)
                ys.append(jnp.dot(m, xt[:, lanes], preferred_element_type=F32))
            y = jnp.where(lane_lo, ys[0], ys[1])
            st = st_ref[:, lanes]
            y = y + jnp.dot(cg, st.astype(BF16), preferred_element_type=F32) * ecum[:, lanes]
            upd = jnp.dot(bgt, xw[:, lanes], preferred_element_type=F32)
            st_ref[:, lanes] = st * sdec[:, lanes] + upd
            y = y + xs[:, lanes] * dx_ref[:, lanes]
            yscr_ref[:, lanes] = y * _silu(z_ref[:, lanes])
    gw = SSM_D_INNER // SSM_GROUPS
    for g in range(SSM_GROUPS):
        blk = yscr_ref[:, g * gw:(g + 1) * gw]
        ms = jnp.mean(blk * blk, axis=-1, keepdims=True)
        y_ref[0, :, g * gw:(g + 1) * gw] = (blk * lax.rsqrt(ms + LN_EPS) * nw_ref[:, g * gw:(g + 1) * gw]).astype(y_ref.dtype)

    @pl.when(c == nchunks - 1)
    def _():
        stout_ref[0] = st_ref[...]


def ssd_scan(xbc, z, dt_raw, state0_t, dt_bias, a_log, d_skip, norm_w, *, valid):
    nseq, length, _ = xbc.shape
    q = SSD_Q
    assert length % q == 0
    nchunks = length // q
    hcount = SSM_HPAD
    assert dt_raw.shape[2] == hcount
    padh = lambda v: jnp.pad(v.astype(F32), (0, hcount - SSM_HEADS))
    dt_bias = padh(dt_bias)
    dtt_raw = jnp.swapaxes(dt_raw, 1, 2)
    a = padh(-jnp.exp(a_log.astype(F32)))
    heads = jnp.arange(hcount)
    e64 = (jnp.arange(SSM_D_INNER)[None, :] // SSM_HEAD_DIM == heads[:, None]).astype(BF16)
    eq = (jnp.arange(SSM_HEADS * q)[None, :] // q == heads[:, None]).astype(BF16)
    ti = jnp.arange(q)
    lincl = (ti[None, :] <= ti[:, None]).astype(BF16)
    uincl = (ti[:, None] <= ti[None, :]).astype(BF16)
    dx = jnp.repeat(d_skip.astype(F32), SSM_HEAD_DIM).reshape(1, SSM_D_INNER)
    full = lambda shape: pl.BlockSpec(shape, lambda s, c: tuple(0 for _ in shape))
    return pl.pallas_call(
        functools.partial(_ssd_kernel, valid=valid, nchunks=nchunks),
        grid=(nseq, nchunks),
        in_specs=[pl.BlockSpec((1, q, SSM_CONV_DIM), lambda s, c: (s, c, 0)),
                  pl.BlockSpec((q, SSM_D_INNER), lambda s, c: (s * nchunks + c, 0)),
                  pl.BlockSpec((1, q, hcount), lambda s, c: (s, c, 0)),
                  pl.BlockSpec((1, hcount, q), lambda s, c: (s, 0, c)),
                  pl.BlockSpec((1, SSM_D_STATE, SSM_D_INNER), lambda s, c: (s, 0, 0)),
                  full((1, hcount)), full((hcount, 1)), full((1, hcount)), full((hcount, 1)),
                  full((1, SSM_D_INNER)), full((1, SSM_D_INNER)),
                  full((hcount, SSM_D_INNER)), full((hcount, SSM_HEADS * q)), full((q, q)), full((q, q))],
        out_specs=[pl.BlockSpec((1, q, SSM_D_INNER), lambda s, c: (s, c, 0)),
                   pl.BlockSpec((1, SSM_D_STATE, SSM_D_INNER), lambda s, c: (s, 0, 0))],
        out_shape=[jax.ShapeDtypeStruct((nseq, length, SSM_D_INNER), BF16),
                   jax.ShapeDtypeStruct((nseq, SSM_D_STATE, SSM_D_INNER), F32)],
        scratch_shapes=[pltpu.VMEM((SSM_D_STATE, SSM_D_INNER), F32), pltpu.VMEM((q, SSM_D_INNER), F32)],
        compiler_params=_cparams(("arbitrary", "arbitrary")),
    )(xbc, z, dt_raw, dtt_raw, state0_t, dt_bias.reshape(1, hcount).astype(F32),
      dt_bias.reshape(hcount, 1).astype(F32), a.reshape(1, hcount), a.reshape(hcount, 1), dx,
      norm_w.reshape(1, SSM_D_INNER).astype(F32), e64, eq, lincl, uincl)


META_E1, META_E2, META_G1, META_G2, META_R1, META_R2 = range(6)


def _router_kernel(x_ref, w_ref, meta_ref, cnt_ref, carry_ref, *, tm, nsteps):
    i = pl.program_id(0)

    @pl.when(i == 0)
    def _():
        carry_ref[...] = jnp.zeros(carry_ref.shape, F32)

    logits = jnp.dot(x_ref[...], w_ref[...], preferred_element_type=F32, precision=lax.Precision.HIGHEST)
    lane = lax.broadcasted_iota(jnp.int32, (tm, LANES), 1).astype(F32)
    logits = jnp.where(lane < N_EXPERTS, logits, -jnp.inf)
    m1 = jnp.max(logits, axis=-1, keepdims=True)
    i1 = jnp.min(jnp.where(logits == m1, lane, float(LANES)), axis=-1, keepdims=True)
    rest = jnp.where(lane == i1, -jnp.inf, logits)
    m2 = jnp.max(rest, axis=-1, keepdims=True)
    i2 = jnp.min(jnp.where(rest == m2, lane, float(LANES)), axis=-1, keepdims=True)
    e2 = jnp.exp(m2 - m1)
    g1 = 1.0 / (1.0 + e2)
    g2 = e2 / (1.0 + e2)
    oh1 = (lane == i1).astype(F32)
    oh2 = (lane == i2).astype(F32)
    both = oh1 + oh2
    r = lax.broadcasted_iota(jnp.int32, (tm, tm), 0)
    cc = lax.broadcasted_iota(jnp.int32, (tm, tm), 1)
    strict = (cc < r).astype(BF16)
    before = jnp.dot(strict, both.astype(BF16), preferred_element_type=F32) + carry_ref[0:1, :]
    r1 = jnp.sum(oh1 * before, axis=-1, keepdims=True)
    r2 = jnp.sum(oh2 * before, axis=-1, keepdims=True)
    carry_ref[0:1, :] = carry_ref[0:1, :] + jnp.sum(both, axis=0, keepdims=True)
    meta = jnp.zeros((tm, LANES), F32)
    for idx, val in ((META_E1, i1), (META_E2, i2), (META_G1, g1), (META_G2, g2), (META_R1, r1), (META_R2, r2)):
        meta = jnp.where(lane == float(idx), val, meta)
    meta_ref[...] = meta

    @pl.when(i == nsteps - 1)
    def _():
        cnt_ref[...] = carry_ref[...]


def moe_router(x, w_router, *, tm):
    rows = x.shape[0]
    assert rows % tm == 0
    nsteps = rows // tm
    wpad = jnp.zeros((D_MODEL, LANES), F32).at[:, :N_EXPERTS].set(w_router.astype(F32))
    return pl.pallas_call(
        functools.partial(_router_kernel, tm=tm, nsteps=nsteps),
        grid=(nsteps,),
        in_specs=[pl.BlockSpec((tm, D_MODEL), lambda i: (i, 0)),
                  pl.BlockSpec((D_MODEL, LANES), lambda i: (0, 0))],
        out_specs=[pl.BlockSpec((tm, LANES), lambda i: (i, 0)),
                   pl.BlockSpec((SUBLANES, LANES), lambda i: (0, 0))],
        out_shape=[jax.ShapeDtypeStruct((rows, LANES), F32), jax.ShapeDtypeStruct((SUBLANES, LANES), F32)],
        scratch_shapes=[pltpu.VMEM((SUBLANES, LANES), F32)],
        compiler_params=_cparams(("arbitrary",)),
    )(x, wpad)


ROW_SLAB = (D_MODEL // LANES, LANES)


def _scatter_kernel(d1_ref, d2_ref, x_ref, init_ref, o_ref, sem, *, tm):
    del init_ref

    def start(r, carry):
        pltpu.make_async_copy(x_ref.at[r], o_ref.at[d1_ref[0, 0, r]], sem).start()
        pltpu.make_async_copy(x_ref.at[r], o_ref.at[d2_ref[0, 0, r]], sem).start()
        return carry

    lax.fori_loop(0, tm, start, 0)

    def wait(r, carry):
        pltpu.make_async_copy(x_ref.at[r], o_ref.at[d1_ref[0, 0, r]], sem).wait()
        pltpu.make_async_copy(x_ref.at[r], o_ref.at[d2_ref[0, 0, r]], sem).wait()
        return carry

    lax.fori_loop(0, tm, wait, 0)


def moe_scatter(x3, d1, d2, rows_out, *, tm):
    rows = x3.shape[0]
    assert rows % tm == 0
    nsteps = rows // tm
    init = jnp.zeros((rows_out,) + ROW_SLAB, x3.dtype)
    smem = lambda: pl.BlockSpec((1, 1, tm), lambda i: (i, 0, 0), memory_space=pltpu.SMEM)
    return pl.pallas_call(
        functools.partial(_scatter_kernel, tm=tm),
        grid=(nsteps,),
        in_specs=[smem(), smem(),
                  pl.BlockSpec((tm,) + ROW_SLAB, lambda i: (i, 0, 0)),
                  pl.BlockSpec(memory_space=pl.ANY)],
        out_specs=pl.BlockSpec(memory_space=pl.ANY),
        out_shape=jax.ShapeDtypeStruct((rows_out,) + ROW_SLAB, x3.dtype),
        scratch_shapes=[pltpu.SemaphoreType.DMA(())],
        input_output_aliases={3: 0},
        compiler_params=_cparams(("arbitrary",)),
    )(d1.reshape(nsteps, 1, tm), d2.reshape(nsteps, 1, tm), x3, init)


def _gather_kernel(d1_ref, d2_ref, y_ref, o1_ref, o2_ref, sem, *, tm):
    def start(r, carry):
        pltpu.make_async_copy(y_ref.at[d1_ref[0, 0, r]], o1_ref.at[r], sem).start()
        pltpu.make_async_copy(y_ref.at[d2_ref[0, 0, r]], o2_ref.at[r], sem).start()
        return carry

    lax.fori_loop(0, tm, start, 0)

    def wait(r, carry):
        pltpu.make_async_copy(y_ref.at[d1_ref[0, 0, r]], o1_ref.at[r], sem).wait()
        pltpu.make_async_copy(y_ref.at[d2_ref[0, 0, r]], o2_ref.at[r], sem).wait()
        return carry

    lax.fori_loop(0, tm, wait, 0)


def moe_gather(y3, d1, d2, *, tm):
    rows = d1.shape[0]
    assert rows % tm == 0
    nsteps = rows // tm
    smem = lambda: pl.BlockSpec((1, 1, tm), lambda i: (i, 0, 0), memory_space=pltpu.SMEM)
    blk = lambda: pl.BlockSpec((tm,) + ROW_SLAB, lambda i: (i, 0, 0))
    shp = jax.ShapeDtypeStruct((rows,) + ROW_SLAB, y3.dtype)
    return pl.pallas_call(
        functools.partial(_gather_kernel, tm=tm),
        grid=(nsteps,),
        in_specs=[smem(), smem(), pl.BlockSpec(memory_space=pl.ANY)],
        out_specs=[blk(), blk()],
        out_shape=[shp, shp],
        scratch_shapes=[pltpu.SemaphoreType.DMA(())],
        compiler_params=_cparams(("arbitrary",)),
    )(d1.reshape(nsteps, 1, tm), d2.reshape(nsteps, 1, tm), y3)


def _combine_ln_kernel(x_ref, y1_ref, y2_ref, meta_ref, g_ref, b_ref, of_ref, ob_ref):
    meta = meta_ref[...]
    g1 = meta[:, META_G1:META_G1 + 1]
    g2 = meta[:, META_G2:META_G2 + 1]
    y = ALPHA * x_ref[...] + (g1 * y1_ref[...] + g2 * y2_ref[...])
    out = _layer_norm_rows(y, g_ref[...], b_ref[...])
    of_ref[...] = out
    ob_ref[...] = out.astype(BF16)


def moe_combine_ln(x, y1, y2, meta, g, b, *, tm):
    rows = x.shape[0]
    assert rows % tm == 0
    row = lambda a: a.reshape(1, D_MODEL).astype(F32)
    blk = lambda w: pl.BlockSpec((tm, w), lambda i: (i, 0))
    vec = pl.BlockSpec((1, D_MODEL), lambda i: (0, 0))
    return pl.pallas_call(
        _combine_ln_kernel,
        grid=(rows // tm,),
        in_specs=[blk(D_MODEL), blk(D_MODEL), blk(D_MODEL), blk(LANES), vec, vec],
        out_specs=[blk(D_MODEL), blk(D_MODEL)],
        out_shape=[jax.ShapeDtypeStruct((rows, D_MODEL), F32), jax.ShapeDtypeStruct((rows, D_MODEL), BF16)],
        compiler_params=_cparams(("arbitrary",)),
    )(x, y1, y2, meta, row(g), row(b))


MOE_TM = 512


def moe_layer(xf, w_router, w_gu, w_down, ln_g, ln_b, *, tm_tok):
    tokens = xf.shape[0]
    meta, counts = moe_router(xf, w_router, tm=ROUTER_TM)
    counts = counts[0, :N_EXPERTS].astype(jnp.int32)
    padded = ((counts + MOE_TM - 1) // MOE_TM) * MOE_TM
    ends = jnp.cumsum(padded)
    starts = ends - padded
    n_tiles = (2 * tokens + N_EXPERTS * (MOE_TM - 1)) // MOE_TM
    rows_out = n_tiles * MOE_TM
    e1 = meta[:, META_E1].astype(jnp.int32)
    e2 = meta[:, META_E2].astype(jnp.int32)
    d1 = starts[e1] + meta[:, META_R1].astype(jnp.int32)
    d2 = starts[e2] + meta[:, META_R2].astype(jnp.int32)
    tile_start = jnp.arange(n_tiles, dtype=jnp.int32) * MOE_TM
    tile_expert = jnp.minimum(jnp.sum((tile_start[:, None] >= ends[None, :]).astype(jnp.int32), axis=1),
                              N_EXPERTS - 1).astype(jnp.int32)
    n_used = (ends[N_EXPERTS - 1] // MOE_TM).astype(jnp.int32).reshape(1)
    xs3 = moe_scatter(xf.reshape((tokens,) + ROW_SLAB), d1, d2, rows_out, tm=tm_tok)
    xs = xs3.reshape(rows_out, D_MODEL)
    (hmid,) = ws_matmul(xs, w_gu, n_out=EXPERT_DIM, tm=MOE_TM, tn=1024, act="swiglu",
                        tile_expert=tile_expert, n_used=n_used, out_dtypes=(BF16,))
    (ys,) = ws_matmul(hmid, w_down, n_out=D_MODEL, tm=MOE_TM, tn=512, act="none",
                      tile_expert=tile_expert, n_used=n_used, out_dtypes=(F32,))
    y1, y2 = moe_gather(ys.reshape((rows_out,) + ROW_SLAB), d1, d2, tm=tm_tok)
    return moe_combine_ln(xf, y1.reshape(tokens, D_MODEL), y2.reshape(tokens, D_MODEL), meta, ln_g, ln_b,
                          tm=tm_tok)


TOK_TM = 1376
LN_TM = 688
LN_TK = 512
ROUTER_TM = 384


def kernel(x_prompt, x_sample, state_cconv, cache_sb_k, cache_sb_v, state_ssm_conv, state_ssm, cc_w_pw1, cc_b_pw1, cc_w_dw, cc_b_dw, cc_ln_g, cc_ln_b, cc_w_pw2, cc_b_pw2, sb_w_qkv, sb_w_o, ssm_w_in, ssm_conv_w, ssm_conv_b, ssm_dt_bias, ssm_A_log, ssm_D, ssm_norm_w, ssm_w_out, ffn_w_gu, ffn_w_down, moe_w_router, moe_w_gu, moe_w_down, ln_g, ln_b):
    bp, seq, d = x_prompt.shape
    bs, lq, _ = x_sample.shape
    n_p = bp * seq
    n_s = bs * lq
    tokens = n_p + n_s
    assert tokens % TOK_TM == 0 and tokens % LN_TM == 0
    xf = jnp.concatenate([x_prompt.reshape(n_p, d), x_sample.reshape(n_s, d)], axis=0)
    xb = xf.astype(BF16)
    zeros_d = jnp.zeros((d,), F32)
    tri = (jnp.arange(SB_SUB)[:, None] > jnp.arange(SB_SUB)[None, :]).astype(BF16)

    new_cc_p, new_cc_s = [], []
    for i in range(DEPTH):
        kind = i % 3
        j = i // 3
        if kind == 0:
            halo = 32
            (u,) = ws_matmul(xb, cc_w_pw1[j].astype(BF16)[None], n_out=d, tm=TOK_TM, tn=512, act="glu",
                             bias=cc_b_pw1[j].reshape(1, 1, 2 * d), out_dtypes=(F32,))
            u_s = u[n_p:].reshape(bs, lq, d)
            hist_p = jnp.zeros((bp, halo, d), F32)
            hist_s = jnp.pad(state_cconv[j], ((0, 0), (halo - (CC_WIDTH - 1), 0), (0, 0)))
            conv = functools.partial(dwconv, u, w=cc_w_dw[j], cbias=cc_b_dw[j], ln_g=cc_ln_g[j], ln_b=cc_ln_b[j],
                                     col0=0, taps=CC_WIDTH, halo=halo, tc=d, mode="ln_silu", out_dtype=BF16)
            c_p = conv(hist_p, row0=0, nseq=bp, length=seq, tl=128)
            c_s = conv(hist_s, row0=n_p, nseq=bs, length=lq, tl=lq)
            c = jnp.concatenate([c_p.reshape(n_p, d), c_s.reshape(n_s, d)], axis=0)
            tail = CC_WIDTH - 1
            new_cc_p.append(jnp.stack([u[(b + 1) * seq - tail:(b + 1) * seq] for b in range(bp)]))
            new_cc_s.append(jnp.concatenate([state_cconv[j], u_s], axis=1)[:, lq:])
            xf, xb = xs_matmul_ln(c, cc_w_pw2[j].astype(BF16), cc_b_pw2[j], xf, ln_g[i, 0], ln_b[i, 0],
                                  tm=LN_TM, tk=LN_TK)
        elif kind == 1:
            qkv_f, qkv_b = ws_matmul(xb, sb_w_qkv[j].astype(BF16)[None], n_out=3 * d, tm=TOK_TM, tn=1024,
                                     out_dtypes=(F32, BF16))
            k_p = qkv_f[:n_p, d:2 * d].reshape(1, bp, seq, SB_HEADS, SB_HEAD_DIM)
            v_p = qkv_f[:n_p, 2 * d:].reshape(1, bp, seq, SB_HEADS, SB_HEAD_DIM)
            k_s = qkv_f[n_p:, d:2 * d].reshape(1, bs, lq, SB_HEADS, SB_HEAD_DIM)
            v_s = qkv_f[n_p:, 2 * d:].reshape(1, bs, lq, SB_HEADS, SB_HEAD_DIM)
            o_p = sb_attention_prompt(qkv_b, tri, nbatch=bp, seq=seq, tq=512, tk=512)
            past = cache_sb_k.shape[2]
            o_s = sb_attention_sample(qkv_b, cache_sb_k[j].reshape(bs, past, d), cache_sb_v[j].reshape(bs, past, d),
                                      tri, row0=n_p, nbatch=bs, lq=lq, tk=512)
            o = jnp.concatenate([o_p, o_s], axis=0)
            xf, xb = xs_matmul_ln(o, sb_w_o[j].astype(BF16), zeros_d, xf, ln_g[i, 0], ln_b[i, 0],
                                  tm=LN_TM, tk=LN_TK)
        else:
            w_in = ssm_w_in[j]
            nzx = SSM_D_INNER + SSM_CONV_DIM
            (zx,) = ws_matmul(xb, w_in[:, :nzx].astype(BF16)[None], n_out=nzx, tm=TOK_TM, tn=1024,
                              out_dtypes=(F32,))
            w_dt = jnp.zeros((d, LANES), BF16).at[:, :SSM_HEADS].set(w_in[:, nzx:].astype(BF16))
            (dt_raw,) = ws_matmul(xb, w_dt[None], n_out=LANES, tm=TOK_TM, tn=LANES, out_dtypes=(F32,))
            xbc_s = zx[n_p:, SSM_D_INNER:].reshape(bs, lq, SSM_CONV_DIM)
            halo = 8
            hist_p = jnp.zeros((bp, halo, SSM_CONV_DIM), F32)
            hist_s = jnp.pad(state_ssm_conv[j], ((0, 0), (halo - (SSM_CONV - 1), 0), (0, 0)))
            conv = functools.partial(dwconv, zx, w=ssm_conv_w[j], cbias=ssm_conv_b[j], ln_g=None, ln_b=None,
                                     col0=SSM_D_INNER, taps=SSM_CONV, halo=halo, tc=2048, mode="silu",
                                     out_dtype=F32)
            a_p = conv(hist_p, row0=0, nseq=bp, length=seq, tl=256)
            a_s = conv(hist_s, row0=n_p, nseq=bs, length=lq, tl=lq)
            pad_s = SSD_Q - lq
            a_s = jnp.pad(a_s, ((0, 0), (0, pad_s), (0, 0)))
            z_p = zx
            z_s = jnp.pad(zx[n_p:, :SSM_D_INNER].reshape(bs, lq, SSM_D_INNER),
                          ((0, 0), (0, pad_s), (0, 0))).reshape(bs * SSD_Q, SSM_D_INNER)
            dt_p = dt_raw[:n_p].reshape(bp, seq, SSM_HPAD)
            dt_s = jnp.pad(dt_raw[n_p:].reshape(bs, lq, SSM_HPAD), ((0, 0), (0, pad_s), (0, 0)))
            st0_p = jnp.zeros((bp, SSM_D_STATE, SSM_D_INNER), F32)
            st0_s = jnp.swapaxes(state_ssm[j].reshape(bs, SSM_D_INNER, SSM_D_STATE), 1, 2)
            scan = functools.partial(ssd_scan, dt_bias=ssm_dt_bias[j], a_log=ssm_A_log[j], d_skip=ssm_D[j],
                                     norm_w=ssm_norm_w[j])
            y_p, st_p = scan(a_p, z_p, dt_p, st0_p, valid=seq)
            y_s, st_s = scan(a_s, z_s, dt_s, st0_s, valid=lq)
            y = jnp.concatenate([y_p.reshape(n_p, SSM_D_INNER), y_s[:, :lq].reshape(n_s, SSM_D_INNER)], axis=0)
            tail = SSM_CONV - 1
            new_sc_p = jnp.stack([zx[(b + 1) * seq - tail:(b + 1) * seq, SSM_D_INNER:] for b in range(bp)])[None]
            new_sc_s = jnp.concatenate([state_ssm_conv[j], xbc_s], axis=1)[:, lq:][None]
            unt = lambda s, n: jnp.swapaxes(s, 1, 2).reshape(1, n, SSM_HEADS, SSM_HEAD_DIM, SSM_D_STATE)
            new_ss_p = unt(st_p, bp)
            new_ss_s = unt(st_s, bs)
            xf, xb = xs_matmul_ln(y, ssm_w_out[j].astype(BF16), zeros_d, xf, ln_g[i, 0], ln_b[i, 0],
                                  tm=LN_TM, tk=LN_TK)
        f = i // 2
        if i % 2 == 0:
            (hmid,) = ws_matmul(xb, ffn_w_gu[f].astype(BF16)[None], n_out=FFN_DIM, tm=TOK_TM, tn=512,
                                act="swiglu", out_dtypes=(BF16,))
            xf, xb = xs_matmul_ln(hmid, ffn_w_down[f].astype(BF16), zeros_d, xf, ln_g[i, 1], ln_b[i, 1],
                                  tm=LN_TM, tk=LN_TK)
        else:
            xf, xb = moe_layer(xf, moe_w_router[f], moe_w_gu[f].astype(BF16), moe_w_down[f].astype(BF16),
                               ln_g[i, 1], ln_b[i, 1], tm_tok=LN_TM)

    y_prompt = xf[:n_p].reshape(bp, seq, d)
    y_sample = xf[n_p:].reshape(bs, lq, d)
    return (y_prompt, y_sample, jnp.stack(new_cc_p), jnp.stack(new_cc_s), k_p, v_p, k_s, v_s,
            new_sc_p, new_sc_s, new_ss_p, new_ss_s)
```

```python
import functools
import math

import jax
import jax.numpy as jnp
from jax import lax
from jax.experimental import pallas as pl
from jax.experimental.pallas import tpu as pltpu

F32 = jnp.float32
BF16 = jnp.bfloat16

D_MODEL = 2048
DEPTH = 4
ALPHA = (2.0 * DEPTH) ** 0.25
LN_EPS = 1e-5
CC_WIDTH = 31
SB_HEADS = 16
SB_HEAD_DIM = 128
SSM_D_INNER = 4096
SSM_HEAD_DIM = 64
SSM_HEADS = 64
SSM_GROUPS = 8
SSM_D_STATE = 128
SSM_CONV = 4
SSM_CONV_DIM = SSM_D_INNER + 2 * SSM_GROUPS * SSM_D_STATE
FFN_DIM = 5632
N_EXPERTS = 8
EXPERT_DIM = 7168

VMEM_LIMIT_BYTES = 56 * 1024 * 1024
LANES = 128
SUBLANES = 8


def _cparams(sem):
    return pltpu.CompilerParams(dimension_semantics=sem, vmem_limit_bytes=VMEM_LIMIT_BYTES)


def _bdot(a, b):
    return jnp.dot(a.astype(BF16), b.astype(BF16), preferred_element_type=F32)


def _split_dot(a, onehot_bf16, passes):
    out = None
    rem = a
    for p in range(passes):
        piece = rem.astype(BF16)
        term = jnp.dot(piece, onehot_bf16, preferred_element_type=F32)
        out = term if out is None else out + term
        if p + 1 < passes:
            rem = rem - piece.astype(F32)
    return out


def _split_dot_left(onehot_bf16, a, passes):
    out = None
    rem = a
    for p in range(passes):
        piece = rem.astype(BF16)
        term = jnp.dot(onehot_bf16, piece, preferred_element_type=F32)
        out = term if out is None else out + term
        if p + 1 < passes:
            rem = rem - piece.astype(F32)
    return out


def _layer_norm_rows(y, g, b):
    mu = jnp.mean(y, axis=-1, keepdims=True)
    yc = y - mu
    var = jnp.mean(yc * yc, axis=-1, keepdims=True)
    return yc * lax.rsqrt(var + LN_EPS) * g + b


def _sigmoid(x):
    return 1.0 / (1.0 + jnp.exp(-x))


def _silu(x):
    return x * _sigmoid(x)


def _softplus(x):
    return jnp.maximum(x, 0.0) + jnp.log(1.0 + jnp.exp(-jnp.abs(x)))


def _ws_kernel(te_ref, nu_ref, *refs, act, has_bias, n_outs):
    del te_ref
    i = pl.program_id(1)
    outs = refs[len(refs) - n_outs:]
    ins = refs[:len(refs) - n_outs]
    x_ref = ins[0]
    two = act in ("glu", "swiglu")

    @pl.when(i < nu_ref[0])
    def _():
        x = x_ref[...].astype(BF16)
        a = jnp.dot(x, ins[1][0], preferred_element_type=F32)
        if two:
            b = jnp.dot(x, ins[2][0], preferred_element_type=F32)
        if has_bias:
            a = a + ins[3 if two else 2][0]
            if two:
                b = b + ins[4][0]
        if act == "glu":
            r = a * _sigmoid(b)
        elif act == "swiglu":
            r = _silu(a) * b
        else:
            r = a
        for o in outs:
            o[...] = r.astype(o.dtype)

    @pl.when(i >= nu_ref[0])
    def _():
        for o in outs:
            o[...] = jnp.zeros(o.shape, o.dtype)


def ws_matmul(x, w, *, n_out, tm, tn, act="none", bias=None, tile_expert=None, n_used=None,
              out_dtypes=(F32,)):
    rows, kdim = x.shape
    assert rows % tm == 0 and n_out % tn == 0
    n_row_tiles = rows // tm
    n_col_tiles = n_out // tn
    two = act in ("glu", "swiglu")
    if tile_expert is None:
        tile_expert = jnp.zeros((n_row_tiles,), jnp.int32)
        n_used = jnp.full((1,), n_row_tiles, jnp.int32)
    in_specs = [pl.BlockSpec((tm, kdim), lambda j, i, te, nu: (i, 0)),
                pl.BlockSpec((1, kdim, tn), lambda j, i, te, nu: (te[i], 0, j))]
    args = [x, w]
    if two:
        in_specs.append(pl.BlockSpec((1, kdim, tn), lambda j, i, te, nu: (te[i], 0, j + n_col_tiles)))
        args.append(w)
    if bias is not None:
        in_specs.append(pl.BlockSpec((1, 1, tn), lambda j, i, te, nu: (te[i], 0, j)))
        args.append(bias)
        if two:
            in_specs.append(pl.BlockSpec((1, 1, tn), lambda j, i, te, nu: (te[i], 0, j + n_col_tiles)))
            args.append(bias)
    out_specs = [pl.BlockSpec((tm, tn), lambda j, i, te, nu: (i, j)) for _ in out_dtypes]
    out_shape = [jax.ShapeDtypeStruct((rows, n_out), dt) for dt in out_dtypes]
    res = pl.pallas_call(
        functools.partial(_ws_kernel, act=act, has_bias=bias is not None, n_outs=len(out_dtypes)),
        grid_spec=pltpu.PrefetchScalarGridSpec(
            num_scalar_prefetch=2, grid=(n_col_tiles, n_row_tiles),
            in_specs=in_specs, out_specs=out_specs),
        out_shape=out_shape,
        compiler_params=_cparams(("arbitrary", "arbitrary")),
    )(tile_expert, n_used, *args)
    return res


def _xs_ln_kernel(x_ref, w_ref, bias_ref, res_ref, g_ref, b_ref, of_ref, ob_ref, acc_ref, *, nk):
    k = pl.program_id(1)

    @pl.when(k == 0)
    def _():
        acc_ref[...] = jnp.zeros(acc_ref.shape, F32)

    acc_ref[...] += jnp.dot(x_ref[...].astype(BF16), w_ref[...], preferred_element_type=F32)

    @pl.when(k == nk - 1)
    def _():
        y = ALPHA * res_ref[...] + (acc_ref[...] + bias_ref[...])
        out = _layer_norm_rows(y, g_ref[...], b_ref[...])
        of_ref[...] = out
        ob_ref[...] = out.astype(BF16)


def xs_matmul_ln(x, w, bias, res, g, b, *, tm, tk):
    rows, kdim = x.shape
    n = w.shape[1]
    assert rows % tm == 0 and kdim % tk == 0 and n == D_MODEL
    nk = kdim // tk
    row = lambda a: a.reshape(1, n).astype(F32)
    return pl.pallas_call(
        functools.partial(_xs_ln_kernel, nk=nk),
        grid=(rows // tm, nk),
        in_specs=[pl.BlockSpec((tm, tk), lambda i, k: (i, k)),
                  pl.BlockSpec((tk, n), lambda i, k: (k, 0)),
                  pl.BlockSpec((1, n), lambda i, k: (0, 0)),
                  pl.BlockSpec((tm, n), lambda i, k: (i, 0)),
                  pl.BlockSpec((1, n), lambda i, k: (0, 0)),
                  pl.BlockSpec((1, n), lambda i, k: (0, 0))],
        out_specs=[pl.BlockSpec((tm, n), lambda i, k: (i, 0)),
                   pl.BlockSpec((tm, n), lambda i, k: (i, 0))],
        out_shape=[jax.ShapeDtypeStruct((rows, n), F32), jax.ShapeDtypeStruct((rows, n), BF16)],
        scratch_shapes=[pltpu.VMEM((tm, n), F32)],
        compiler_params=_cparams(("arbitrary", "arbitrary")),
    )(x, w, row(bias), res, row(g), row(b))


CONV_ROW_CHUNK = 32
CONV_LANE_CHUNK = 512


def _dwconv_kernel(hist_ref, prev_ref, cur_ref, w_ref, cb_ref, g_ref, b_ref, o_ref, win_ref, y_ref, sh_ref,
                   *, taps, halo, tl, mode, single_step, preshift):
    i = pl.program_id(1)

    @pl.when(i == 0)
    def _():
        win_ref[0:halo, :] = hist_ref[0]

    if not single_step:
        @pl.when(i > 0)
        def _():
            win_ref[0:halo, :] = prev_ref[...]

    win_ref[halo:halo + tl, :] = cur_ref[...]
    off = halo - (taps - 1)
    width = cur_ref.shape[1]
    rc = min(CONV_ROW_CHUNK, tl)
    lc = min(CONV_LANE_CHUNK, width)
    span = halo + tl - SUBLANES
    if preshift:
        for r in range(1, SUBLANES):
            for c0 in range(0, width, lc):
                sh_ref[r - 1, :, c0:c0 + lc] = win_ref[r:r + span, c0:c0 + lc]

    def tap_rows(k, r0, c0):
        s = off + k
        r = s % SUBLANES
        if not preshift or r == 0:
            return win_ref[s + r0:s + r0 + rc, c0:c0 + lc]
        return sh_ref[r - 1, s - r + r0:s - r + r0 + rc, c0:c0 + lc]

    for r0 in range(0, tl, rc):
        for c0 in range(0, width, lc):
            acc = jnp.zeros((rc, lc), F32) + cb_ref[:, c0:c0 + lc]
            for k in range(taps):
                acc = acc + w_ref[k:k + 1, c0:c0 + lc] * tap_rows(k, r0, c0)
            if mode == "silu":
                o_ref[0, r0:r0 + rc, c0:c0 + lc] = _silu(acc).astype(o_ref.dtype)
            else:
                y_ref[r0:r0 + rc, c0:c0 + lc] = acc
    if mode == "ln_silu":
        y = _layer_norm_rows(y_ref[...], g_ref[...], b_ref[...])
        o_ref[0] = _silu(y).astype(o_ref.dtype)


def dwconv(x, hist, w, cbias, ln_g, ln_b, *, row0, nseq, length, col0, taps, halo, tl, tc, mode, out_dtype):
    chans = hist.shape[2]
    assert length % tl == 0 and chans % tc == 0 and row0 % tl == 0 and col0 % tc == 0
    single_step = length == tl
    cur0 = row0 // tl
    cb0 = col0 // tc
    steps = length // tl
    if single_step:
        prev_arr = hist
        prev_spec = pl.BlockSpec((1, halo, tc), lambda s, i, c: (s, 0, c))
    else:
        assert tl % halo == 0 and row0 % halo == 0
        per = tl // halo
        prev_arr = x
        prev_spec = pl.BlockSpec(
            (halo, tc), lambda s, i, c: (jnp.maximum(row0 // halo + (s * steps + i) * per - 1, 0), cb0 + c))
    row = lambda a: a.reshape(1, chans).astype(F32)
    preshift = taps > SUBLANES
    shifted = (SUBLANES - 1, halo + tl - SUBLANES, tc) if preshift else (1, SUBLANES, LANES)
    if ln_g is None:
        ln_g = jnp.ones((chans,), F32)
        ln_b = jnp.zeros((chans,), F32)
    return pl.pallas_call(
        functools.partial(_dwconv_kernel, taps=taps, halo=halo, tl=tl, mode=mode, single_step=single_step,
                          preshift=preshift),
        grid=(nseq, steps, chans // tc),
        in_specs=[pl.BlockSpec((1, halo, tc), lambda s, i, c: (s, 0, c)),
                  prev_spec,
                  pl.BlockSpec((tl, tc), lambda s, i, c: (cur0 + s * steps + i, cb0 + c)),
                  pl.BlockSpec((taps, tc), lambda s, i, c: (0, c)),
                  pl.BlockSpec((1, tc), lambda s, i, c: (0, c)),
                  pl.BlockSpec((1, tc), lambda s, i, c: (0, c)),
                  pl.BlockSpec((1, tc), lambda s, i, c: (0, c))],
        out_specs=pl.BlockSpec((1, tl, tc), lambda s, i, c: (s, i, c)),
        out_shape=jax.ShapeDtypeStruct((nseq, length, chans), out_dtype),
        scratch_shapes=[pltpu.VMEM((halo + tl, tc), F32), pltpu.VMEM((tl, tc), F32), pltpu.VMEM(shifted, F32)],
        compiler_params=_cparams(("arbitrary", "arbitrary", "arbitrary")),
    )(hist, prev_arr, x, w.astype(F32), row(cbias), row(ln_g), row(ln_b))


SB_SUB = 256
SB_QROWS = 256


def _sb_block(q, kb, vb, tri, carry, mask):
    log_beta, suffix, total = _sb_scores(q, kb, tri, mask)
    contrib = _sb_weighted(log_beta, suffix, carry, vb, mask)
    return contrib, carry + total


def _sb_scores(q, kb, tri, mask):
    z = lax.dot_general(q, kb, (((1,), (1,)), ((), ())), preferred_element_type=F32)
    z = z * (SB_HEAD_DIM ** -0.5)
    sp = _softplus(z)
    log_keep = -sp
    if mask is not None:
        log_keep = jnp.where(mask, log_keep, 0.0)
    suffix = jnp.dot(log_keep.astype(BF16), tri, preferred_element_type=F32)
    return z - sp, suffix, suffix[:, 0:1] + log_keep[:, 0:1]


def _sb_weighted(log_beta, suffix, carry, vb, mask):
    w = jnp.exp(log_beta + suffix + carry)
    if mask is not None:
        w = jnp.where(mask, w, 0.0)
    return jnp.dot(w.astype(BF16), vb, preferred_element_type=F32)


def _sb_prompt_kernel(qi_ref, kj_ref, q_ref, k_ref, v_ref, tri_ref, o_ref, acc_ref, carry_ref, *, tq, tk):
    p = pl.program_id(2)
    qi = qi_ref[p]
    kj = kj_ref[p]

    @pl.when(kj == qi)
    def _():
        acc_ref[...] = jnp.zeros(acc_ref.shape, F32)
        carry_ref[...] = jnp.zeros(carry_ref.shape, F32)

    def step(masked):
        tri = tri_ref[...]
        nsub = tk // SB_SUB
        for r0 in range(0, tq, SB_QROWS):
            rows = slice(r0, r0 + SB_QROWS)
            q = q_ref[rows, :]
            acc = acc_ref[rows, :]
            carry = carry_ref[rows, :]
            stats = {}
            for s in range(nsub):
                mask = None
                if masked:
                    if s * SB_SUB >= r0 + SB_QROWS:
                        continue
                    q_pos = r0 + lax.broadcasted_iota(jnp.int32, (SB_QROWS, SB_SUB), 0)
                    k_pos = s * SB_SUB + lax.broadcasted_iota(jnp.int32, (SB_QROWS, SB_SUB), 1)
                    mask = k_pos < q_pos
                stats[s] = _sb_scores(q, k_ref[s * SB_SUB:(s + 1) * SB_SUB, :], tri, mask) + (mask,)
            for s in sorted(stats, reverse=True):
                log_beta, suffix, total, mask = stats[s]
                acc = acc + _sb_weighted(log_beta, suffix, carry, v_ref[s * SB_SUB:(s + 1) * SB_SUB, :], mask)
                carry = carry + total
            acc_ref[rows, :] = acc
            carry_ref[rows, :] = carry

    @pl.when(kj == qi)
    def _():
        step(True)

    @pl.when(kj != qi)
    def _():
        step(False)

    @pl.when(kj == 0)
    def _():
        o_ref[...] = acc_ref[...].astype(o_ref.dtype)


def sb_attention_prompt(qkv, tri, *, nbatch, seq, tq, tk):
    assert tq == tk and seq % tq == 0 and tk % SB_SUB == 0
    nq = seq // tq
    pairs = [(a, b) for a in range(nq) for b in range(a, -1, -1)]
    qi = jnp.asarray([a for a, _ in pairs], jnp.int32)
    kj = jnp.asarray([b for _, b in pairs], jnp.int32)
    h = SB_HEADS
    return pl.pallas_call(
        functools.partial(_sb_prompt_kernel, tq=tq, tk=tk),
        grid_spec=pltpu.PrefetchScalarGridSpec(
            num_scalar_prefetch=2, grid=(nbatch, h, len(pairs)),
            in_specs=[pl.BlockSpec((tq, SB_HEAD_DIM), lambda b, hh, p, qi, kj: (b * nq + qi[p], hh)),
                      pl.BlockSpec((tk, SB_HEAD_DIM), lambda b, hh, p, qi, kj: (b * nq + kj[p], h + hh)),
                      pl.BlockSpec((tk, SB_HEAD_DIM), lambda b, hh, p, qi, kj: (b * nq + kj[p], 2 * h + hh)),
                      pl.BlockSpec((SB_SUB, SB_SUB), lambda b, hh, p, qi, kj: (0, 0))],
            out_specs=pl.BlockSpec((tq, SB_HEAD_DIM), lambda b, hh, p, qi, kj: (b * nq + qi[p], hh)),
            scratch_shapes=[pltpu.VMEM((tq, SB_HEAD_DIM), F32), pltpu.VMEM((tq, 1), F32)]),
        out_shape=jax.ShapeDtypeStruct((nbatch * seq, D_MODEL), BF16),
        compiler_params=_cparams(("arbitrary", "arbitrary", "arbitrary")),
    )(qi, kj, qkv, qkv, qkv, tri)


def _sb_sample_kernel(q_ref, kn_ref, vn_ref, kp_ref, vp_ref, tri_ref, o_ref, acc_ref, carry_ref,
                      *, lq, tk, nsteps):
    b = pl.program_id(0)
    s = pl.program_id(2)
    q = q_ref[...]
    tri = tri_ref[...]

    @pl.when(s == 0)
    def _():
        n = kn_ref.shape[0]
        row = lax.broadcasted_iota(jnp.int32, (lq, n), 0)
        col = lax.broadcasted_iota(jnp.int32, (lq, n), 1)
        mask = (col >= b * lq) & (col < b * lq + row)
        contrib, carry = _sb_block(q, kn_ref[...], vn_ref[...], tri[0:n, 0:n], jnp.zeros((lq, 1), F32), mask)
        acc_ref[...] = contrib
        carry_ref[...] = carry

    @pl.when(s > 0)
    def _():
        acc = acc_ref[...]
        carry = carry_ref[...]
        for u in range(tk // SB_SUB - 1, -1, -1):
            kb = kp_ref[0, u * SB_SUB:(u + 1) * SB_SUB, :].astype(BF16)
            vb = vp_ref[0, u * SB_SUB:(u + 1) * SB_SUB, :].astype(BF16)
            contrib, carry = _sb_block(q, kb, vb, tri, carry, None)
            acc = acc + contrib
        acc_ref[...] = acc
        carry_ref[...] = carry

    @pl.when(s == nsteps - 1)
    def _():
        o_ref[...] = acc_ref[...].astype(o_ref.dtype)


def sb_attention_sample(qkv, past_k, past_v, tri, *, row0, nbatch, lq, tk):
    past = past_k.shape[1]
    nnew = nbatch * lq
    assert past % tk == 0 and row0 % nnew == 0 and row0 % lq == 0 and nnew <= SB_SUB
    nsteps = 1 + past // tk
    h = SB_HEADS
    nb = past // tk
    return pl.pallas_call(
        functools.partial(_sb_sample_kernel, lq=lq, tk=tk, nsteps=nsteps),
        grid=(nbatch, h, nsteps),
        in_specs=[pl.BlockSpec((lq, SB_HEAD_DIM), lambda b, hh, s: (row0 // lq + b, hh)),
                  pl.BlockSpec((nnew, SB_HEAD_DIM), lambda b, hh, s: (row0 // nnew, h + hh)),
                  pl.BlockSpec((nnew, SB_HEAD_DIM), lambda b, hh, s: (row0 // nnew, 2 * h + hh)),
                  pl.BlockSpec((1, tk, SB_HEAD_DIM), lambda b, hh, s: (b, jnp.minimum(nb - s, nb - 1), hh)),
                  pl.BlockSpec((1, tk, SB_HEAD_DIM), lambda b, hh, s: (b, jnp.minimum(nb - s, nb - 1), hh)),
                  pl.BlockSpec((SB_SUB, SB_SUB), lambda b, hh, s: (0, 0))],
        out_specs=pl.BlockSpec((lq, SB_HEAD_DIM), lambda b, hh, s: (b, hh)),
        out_shape=jax.ShapeDtypeStruct((nnew, D_MODEL), BF16),
        scratch_shapes=[pltpu.VMEM((lq, SB_HEAD_DIM), F32), pltpu.VMEM((lq, 1), F32)],
        compiler_params=_cparams(("arbitrary", "arbitrary", "arbitrary")),
    )(qkv, qkv, qkv, past_k, past_v, tri)


SSD_Q = 128
SSM_HPAD = 128


def _ssd_kernel(xbc_ref, z_ref, dt_ref, dtt_ref, st0_ref, dtb_ref, dtbc_ref, a_ref, ac_ref, dx_ref, nw_ref,
                e64_ref, eq_ref, lincl_ref, uincl_ref, y_ref, stout_ref, st_ref, yscr_ref, *, valid, nchunks):
    c = pl.program_id(1)
    q = SSD_Q
    hd = SSM_HEAD_DIM
    pair = 2 * hd

    @pl.when(c == 0)
    def _():
        st_ref[...] = st0_ref[0]

    row_t = lax.broadcasted_iota(jnp.int32, (q, SSM_HPAD), 0) + c * q
    dt = jnp.where(row_t < valid, _softplus(dt_ref[0] + dtb_ref[...]), 0.0)
    col_t = lax.broadcasted_iota(jnp.int32, (SSM_HPAD, q), 1) + c * q
    dtt = jnp.where(col_t < valid, _softplus(dtt_ref[0] + dtbc_ref[...]), 0.0)
    cum = _split_dot_left(lincl_ref[...], dt * a_ref[...], 3)
    cumt = _split_dot(dtt * ac_ref[...], uincl_ref[...], 3)
    e64 = e64_ref[...]
    dtx = _split_dot(dt, e64, 2)
    cumx = _split_dot(cum, e64, 3)
    colc = _split_dot(cum, eq_ref[...], 3)
    clx = cumx[q - 1:q, :]
    xs = xbc_ref[0, :, 0:SSM_D_INNER]
    xdt = xs * dtx
    xt = xdt.astype(BF16)
    xw = (xdt * jnp.exp(clx - cumx)).astype(BF16)
    ecum = jnp.exp(cumx)
    sdec = jnp.exp(clx)
    causal = lax.broadcasted_iota(jnp.int32, (q, q), 1) <= lax.broadcasted_iota(jnp.int32, (q, q), 0)
    lane_lo = lax.broadcasted_iota(jnp.int32, (q, pair), 1) < hd
    gn = SSM_GROUPS * SSM_D_STATE
    hpg = SSM_HEADS // SSM_GROUPS
    for g in range(SSM_GROUPS):
        bg_f = xbc_ref[0, :, SSM_D_INNER + g * SSM_D_STATE:SSM_D_INNER + (g + 1) * SSM_D_STATE]
        bg = bg_f.astype(BF16)
        bgt = bg_f.T.astype(BF16)
        cg =xbc_ref[0, :, SSM_D_INNER + gn + g * SSM_D_STATE:SSM_D_INNER + gn + (g + 1) * SSM_D_STATE].astype(BF16)
        cb = lax.dot_general(cg, bg, (((1,), (1,)), ((), ())), preferred_element_type=F32)
        for j in range(g * hpg // 2, (g + 1) * hpg // 2):
            lanes = slice(j * pair, (j + 1) * pair)
            ys = []
            for hh in range(2):
                h = 2 * j + hh
                seg = colc[:, h * q:(h + 1) * q] - cumt[h:h + 1, :]
                m = jnp.where(causal, cb * jnp.exp(seg), 0.0).astype(BF16)
                ys.append(jnp.dot(m, xt[:, lanes], preferred_element_type=F32))
            y = jnp.where(lane_lo, ys[0], ys[1])
            st = st_ref[:, lanes]
            y = y + jnp.dot(cg, st.astype(BF16), preferred_element_type=F32) * ecum[:, lanes]
            upd = jnp.dot(bgt, xw[:, lanes], preferred_element_type=F32)
            st_ref[:, lanes] = st * sdec[:, lanes] + upd
            y = y + xs[:, lanes] * dx_ref[:, lanes]
            yscr_ref[:, lanes] = y * _silu(z_ref[:, lanes])
    gw = SSM_D_INNER // SSM_GROUPS
    for g in range(SSM_GROUPS):
        blk = yscr_ref[:, g * gw:(g + 1) * gw]
        ms = jnp.mean(blk * blk, axis=-1, keepdims=True)
        y_ref[0, :, g * gw:(g + 1) * gw] = (blk * lax.rsqrt(ms + LN_EPS) * nw_ref[:, g * gw:(g + 1) * gw]).astype(y_ref.dtype)

    @pl.when(c == nchunks - 1)
    def _():
        stout_ref[0] = st_ref[...]


def ssd_scan(xbc, z, dt_raw, state0_t, dt_bias, a_log, d_skip, norm_w, *, valid):
    nseq, length, _ = xbc.shape
    q = SSD_Q
    assert length % q == 0
    nchunks = length // q
    hcount = SSM_HPAD
    assert dt_raw.shape[2] == hcount
    padh = lambda v: jnp.pad(v.astype(F32), (0, hcount - SSM_HEADS))
    dt_bias = padh(dt_bias)
    dtt_raw = jnp.swapaxes(dt_raw, 1, 2)
    a = padh(-jnp.exp(a_log.astype(F32)))
    heads = jnp.arange(hcount)
    e64 = (jnp.arange(SSM_D_INNER)[None, :] // SSM_HEAD_DIM == heads[:, None]).astype(BF16)
    eq = (jnp.arange(SSM_HEADS * q)[None, :] // q == heads[:, None]).astype(BF16)
    ti = jnp.arange(q)
    lincl = (ti[None, :] <= ti[:, None]).astype(BF16)
    uincl = (ti[:, None] <= ti[None, :]).astype(BF16)
    dx = jnp.repeat(d_skip.astype(F32), SSM_HEAD_DIM).reshape(1, SSM_D_INNER)
    full = lambda shape: pl.BlockSpec(shape, lambda s, c: tuple(0 for _ in shape))
    return pl.pallas_call(
        functools.partial(_ssd_kernel, valid=valid, nchunks=nchunks),
        grid=(nseq, nchunks),
        in_specs=[pl.BlockSpec((1, q, SSM_CONV_DIM), lambda s, c: (s, c, 0)),
                  pl.BlockSpec((q, SSM_D_INNER), lambda s, c: (s * nchunks + c, 0)),
                  pl.BlockSpec((1, q, hcount), lambda s, c: (s, c, 0)),
                  pl.BlockSpec((1, hcount, q), lambda s, c: (s, 0, c)),
                  pl.BlockSpec((1, SSM_D_STATE, SSM_D_INNER), lambda s, c: (s, 0, 0)),
                  full((1, hcount)), full((hcount, 1)), full((1, hcount)), full((hcount, 1)),
                  full((1, SSM_D_INNER)), full((1, SSM_D_INNER)),
                  full((hcount, SSM_D_INNER)), full((hcount, SSM_HEADS * q)), full((q, q)), full((q, q))],
        out_specs=[pl.BlockSpec((1, q, SSM_D_INNER), lambda s, c: (s, c, 0)),
                   pl.BlockSpec((1, SSM_D_STATE, SSM_D_INNER), lambda s, c: (s, 0, 0))],
        out_shape=[jax.ShapeDtypeStruct((nseq, length, SSM_D_INNER), BF16),
                   jax.ShapeDtypeStruct((nseq, SSM_D_STATE, SSM_D_INNER), F32)],
        scratch_shapes=[pltpu.VMEM((SSM_D_STATE, SSM_D_INNER), F32), pltpu.VMEM((q, SSM_D_INNER), F32)],
        compiler_params=_cparams(("arbitrary", "arbitrary")),
    )(xbc, z, dt_raw, dtt_raw, state0_t, dt_bias.reshape(1, hcount).astype(F32),
      dt_bias.reshape(hcount, 1).astype(F32), a.reshape(1, hcount), a.reshape(hcount, 1), dx,
      norm_w.reshape(1, SSM_D_INNER).astype(F32), e64, eq, lincl, uincl)


META_E1, META_E2, META_G1, META_G2, META_R1, META_R2 = range(6)


def _router_kernel(x_ref, w_ref, meta_ref, cnt_ref, carry_ref, *, tm, nsteps):
    i = pl.program_id(0)

    @pl.when(i == 0)
    def _():
        carry_ref[...] = jnp.zeros(carry_ref.shape, F32)

    logits = jnp.dot(x_ref[...], w_ref[...], preferred_element_type=F32, precision=lax.Precision.HIGHEST)
    lane = lax.broadcasted_iota(jnp.int32, (tm, LANES), 1).astype(F32)
    logits = jnp.where(lane < N_EXPERTS, logits, -jnp.inf)
    m1 = jnp.max(logits, axis=-1, keepdims=True)
    i1 = jnp.min(jnp.where(logits == m1, lane, float(LANES)), axis=-1, keepdims=True)
    rest = jnp.where(lane == i1, -jnp.inf, logits)
    m2 = jnp.max(rest, axis=-1, keepdims=True)
    i2 = jnp.min(jnp.where(rest == m2, lane, float(LANES)), axis=-1, keepdims=True)
    e2 = jnp.exp(m2 - m1)
    g1 = 1.0 / (1.0 + e2)
    g2 = e2 / (1.0 + e2)
    oh1 = (lane == i1).astype(F32)
    oh2 = (lane == i2).astype(F32)
    both = oh1 + oh2
    r = lax.broadcasted_iota(jnp.int32, (tm, tm), 0)
    cc = lax.broadcasted_iota(jnp.int32, (tm, tm), 1)
    strict = (cc < r).astype(BF16)
    before = jnp.dot(strict, both.astype(BF16), preferred_element_type=F32) + carry_ref[0:1, :]
    r1 = jnp.sum(oh1 * before, axis=-1, keepdims=True)
    r2 = jnp.sum(oh2 * before, axis=-1, keepdims=True)
    carry_ref[0:1, :] = carry_ref[0:1, :] + jnp.sum(both, axis=0, keepdims=True)
    meta = jnp.zeros((tm, LANES), F32)
    for idx, val in ((META_E1, i1), (META_E2, i2), (META_G1, g1), (META_G2, g2), (META_R1, r1), (META_R2, r2)):
        meta = jnp.where(lane == float(idx), val, meta)
    meta_ref[...] = meta

    @pl.when(i == nsteps - 1)
    def _():
        cnt_ref[...] = carry_ref[...]


def moe_router(x, w_router, *, tm):
    rows = x.shape[0]
    assert rows % tm == 0
    nsteps = rows // tm
    wpad = jnp.zeros((D_MODEL, LANES), F32).at[:, :N_EXPERTS].set(w_router.astype(F32))
    return pl.pallas_call(
        functools.partial(_router_kernel, tm=tm, nsteps=nsteps),
        grid=(nsteps,),
        in_specs=[pl.BlockSpec((tm, D_MODEL), lambda i: (i, 0)),
                  pl.BlockSpec((D_MODEL, LANES), lambda i: (0, 0))],
        out_specs=[pl.BlockSpec((tm, LANES), lambda i: (i, 0)),
                   pl.BlockSpec((SUBLANES, LANES), lambda i: (0, 0))],
        out_shape=[jax.ShapeDtypeStruct((rows, LANES), F32), jax.ShapeDtypeStruct((SUBLANES, LANES), F32)],
        scratch_shapes=[pltpu.VMEM((SUBLANES, LANES), F32)],
        compiler_params=_cparams(("arbitrary",)),
    )(x, wpad)


ROW_SLAB = (D_MODEL // LANES, LANES)


def _scatter_kernel(d1_ref, d2_ref, x_ref, init_ref, o_ref, sem, *, tm):
    del init_ref

    def start(r, carry):
        pltpu.make_async_copy(x_ref.at[r], o_ref.at[d1_ref[0, 0, r]], sem).start()
        pltpu.make_async_copy(x_ref.at[r], o_ref.at[d2_ref[0, 0, r]], sem).start()
        return carry

    lax.fori_loop(0, tm, start, 0)

    def wait(r, carry):
        pltpu.make_async_copy(x_ref.at[r], o_ref.at[d1_ref[0, 0, r]], sem).wait()
        pltpu.make_async_copy(x_ref.at[r], o_ref.at[d2_ref[0, 0, r]], sem).wait()
        return carry

    lax.fori_loop(0, tm, wait, 0)


def moe_scatter(x3, d1, d2, rows_out, *, tm):
    rows = x3.shape[0]
    assert rows % tm == 0
    nsteps = rows // tm
    init = jnp.zeros((rows_out,) + ROW_SLAB, x3.dtype)
    smem = lambda: pl.BlockSpec((1, 1, tm), lambda i: (i, 0, 0), memory_space=pltpu.SMEM)
    return pl.pallas_call(
        functools.partial(_scatter_kernel, tm=tm),
        grid=(nsteps,),
        in_specs=[smem(), smem(),
                  pl.BlockSpec((tm,) + ROW_SLAB, lambda i: (i, 0, 0)),
                  pl.BlockSpec(memory_space=pl.ANY)],
        out_specs=pl.BlockSpec(memory_space=pl.ANY),
        out_shape=jax.ShapeDtypeStruct((rows_out,) + ROW_SLAB, x3.dtype),
        scratch_shapes=[pltpu.SemaphoreType.DMA(())],
        input_output_aliases={3: 0},
        compiler_params=_cparams(("arbitrary",)),
    )(d1.reshape(nsteps, 1, tm), d2.reshape(nsteps, 1, tm), x3, init)


def _gather_kernel(d1_ref, d2_ref, y_ref, o1_ref, o2_ref, sem, *, tm):
    def start(r, carry):
        pltpu.make_async_copy(y_ref.at[d1_ref[0, 0, r]], o1_ref.at[r], sem).start()
        pltpu.make_async_copy(y_ref.at[d2_ref[0, 0, r]], o2_ref.at[r], sem).start()
        return carry

    lax.fori_loop(0, tm, start, 0)

    def wait(r, carry):
        pltpu.make_async_copy(y_ref.at[d1_ref[0, 0, r]], o1_ref.at[r], sem).wait()
        pltpu.make_async_copy(y_ref.at[d2_ref[0, 0, r]], o2_ref.at[r], sem).wait()
        return carry

    lax.fori_loop(0, tm, wait, 0)


def moe_gather(y3, d1, d2, *, tm):
    rows = d1.shape[0]
    assert rows % tm == 0
    nsteps = rows // tm
    smem = lambda: pl.BlockSpec((1, 1, tm), lambda i: (i, 0, 0), memory_space=pltpu.SMEM)
    blk = lambda: pl.BlockSpec((tm,) + ROW_SLAB, lambda i: (i, 0, 0))
    shp = jax.ShapeDtypeStruct((rows,) + ROW_SLAB, y3.dtype)
    return pl.pallas_call(
        functools.partial(_gather_kernel, tm=tm),
        grid=(nsteps,),
        in_specs=[smem(), smem(), pl.BlockSpec(memory_space=pl.ANY)],
        out_specs=[blk(), blk()],
        out_shape=[shp, shp],
        scratch_shapes=[pltpu.SemaphoreType.DMA(())],
        compiler_params=_cparams(("arbitrary",)),
    )(d1.reshape(nsteps, 1, tm), d2.reshape(nsteps, 1, tm), y3)


def _combine_ln_kernel(x_ref, y1_ref, y2_ref, meta_ref, g_ref, b_ref, of_ref, ob_ref):
    meta = meta_ref[...]
    g1 = meta[:, META_G1:META_G1 + 1]
    g2 = meta[:, META_G2:META_G2 + 1]
    y = ALPHA * x_ref[...] + (g1 * y1_ref[...] + g2 * y2_ref[...])
    out = _layer_norm_rows(y, g_ref[...], b_ref[...])
    of_ref[...] = out
    ob_ref[...] = out.astype(BF16)


def moe_combine_ln(x, y1, y2, meta, g, b, *, tm):
    rows = x.shape[0]
    assert rows % tm == 0
    row = lambda a: a.reshape(1, D_MODEL).astype(F32)
    blk = lambda w: pl.BlockSpec((tm, w), lambda i: (i, 0))
    vec = pl.BlockSpec((1, D_MODEL), lambda i: (0, 0))
    return pl.pallas_call(
        _combine_ln_kernel,
        grid=(rows // tm,),
        in_specs=[blk(D_MODEL), blk(D_MODEL), blk(D_MODEL), blk(LANES), vec, vec],
        out_specs=[blk(D_MODEL), blk(D_MODEL)],
        out_shape=[jax.ShapeDtypeStruct((rows, D_MODEL), F32), jax.ShapeDtypeStruct((rows, D_MODEL), BF16)],
        compiler_params=_cparams(("arbitrary",)),
    )(x, y1, y2, meta, row(g), row(b))


MOE_TM = 512


def moe_layer(xf, w_router, w_gu, w_down, ln_g, ln_b, *, tm_tok):
    tokens = xf.shape[0]
    meta, counts = moe_router(xf, w_router, tm=ROUTER_TM)
    counts = counts[0, :N_EXPERTS].astype(jnp.int32)
    padded = ((counts + MOE_TM - 1) // MOE_TM) * MOE_TM
    ends = jnp.cumsum(padded)
    starts = ends - padded
    n_tiles = (2 * tokens + N_EXPERTS * (MOE_TM - 1)) // MOE_TM
    rows_out = n_tiles * MOE_TM
    e1 = meta[:, META_E1].astype(jnp.int32)
    e2 = meta[:, META_E2].astype(jnp.int32)
    d1 = starts[e1] + meta[:, META_R1].astype(jnp.int32)
    d2 = starts[e2] + meta[:, META_R2].astype(jnp.int32)
    tile_start = jnp.arange(n_tiles, dtype=jnp.int32) * MOE_TM
    tile_expert = jnp.minimum(jnp.sum((tile_start[:, None] >= ends[None, :]).astype(jnp.int32), axis=1),
                              N_EXPERTS - 1).astype(jnp.int32)
    n_used = (ends[N_EXPERTS - 1] // MOE_TM).astype(jnp.int32).reshape(1)
    xs3 = moe_scatter(xf.reshape((tokens,) + ROW_SLAB), d1, d2, rows_out, tm=tm_tok)
    xs = xs3.reshape(rows_out, D_MODEL)
    (hmid,) = ws_matmul(xs, w_gu, n_out=EXPERT_DIM, tm=MOE_TM, tn=1024, act="swiglu",
                        tile_expert=tile_expert, n_used=n_used, out_dtypes=(BF16,))
    (ys,) = ws_matmul(hmid, w_down, n_out=D_MODEL, tm=MOE_TM, tn=512, act="none",
                      tile_expert=tile_expert, n_used=n_used, out_dtypes=(F32,))
    y1, y2 = moe_gather(ys.reshape((rows_out,) + ROW_SLAB), d1, d2, tm=tm_tok)
    return moe_combine_ln(xf, y1.reshape(tokens, D_MODEL), y2.reshape(tokens, D_MODEL), meta, ln_g, ln_b,
                          tm=tm_tok)


TOK_TM = 1376
LN_TM = 688
LN_TK = 512
ROUTER_TM = 384


def kernel(x_prompt, x_sample, state_cconv, cache_sb_k, cache_sb_v, state_ssm_conv, state_ssm, cc_w_pw1, cc_b_pw1, cc_w_dw, cc_b_dw, cc_ln_g, cc_ln_b, cc_w_pw2, cc_b_pw2, sb_w_qkv, sb_w_o, ssm_w_in, ssm_conv_w, ssm_conv_b, ssm_dt_bias, ssm_A_log, ssm_D, ssm_norm_w, ssm_w_out, ffn_w_gu, ffn_w_down, moe_w_router, moe_w_gu, moe_w_down, ln_g, ln_b):
    bp, seq, d = x_prompt.shape
    bs, lq, _ = x_sample.shape
    n_p = bp * seq
    n_s = bs * lq
    tokens = n_p + n_s
    assert tokens % TOK_TM == 0 and tokens % LN_TM == 0
    xf = jnp.concatenate([x_prompt.reshape(n_p, d), x_sample.reshape(n_s, d)], axis=0)
    xb = xf.astype(BF16)
    zeros_d = jnp.zeros((d,), F32)
    tri = (jnp.arange(SB_SUB)[:, None] > jnp.arange(SB_SUB)[None, :]).astype(BF16)

    new_cc_p, new_cc_s = [], []
    for i in range(DEPTH):
        kind = i % 3
        j = i // 3
        if kind == 0:
            halo = 32
            (u,) = ws_matmul(xb, cc_w_pw1[j].astype(BF16)[None], n_out=d, tm=TOK_TM, tn=512, act="glu",
                             bias=cc_b_pw1[j].reshape(1, 1, 2 * d), out_dtypes=(F32,))
            u_s = u[n_p:].reshape(bs, lq, d)
            hist_p = jnp.zeros((bp, halo, d), F32)
            hist_s = jnp.pad(state_cconv[j], ((0, 0), (halo - (CC_WIDTH - 1), 0), (0, 0)))
            conv = functools.partial(dwconv, u, w=cc_w_dw[j], cbias=cc_b_dw[j], ln_g=cc_ln_g[j], ln_b=cc_ln_b[j],
                                     col0=0, taps=CC_WIDTH, halo=halo, tc=d, mode="ln_silu", out_dtype=BF16)
            c_p = conv(hist_p, row0=0, nseq=bp, length=seq, tl=128)
            c_s = conv(hist_s, row0=n_p, nseq=bs, length=lq, tl=lq)
            c = jnp.concatenate([c_p.reshape(n_p, d), c_s.reshape(n_s, d)], axis=0)
            tail = CC_WIDTH - 1
            new_cc_p.append(jnp.stack([u[(b + 1) * seq - tail:(b + 1) * seq] for b in range(bp)]))
            new_cc_s.append(jnp.concatenate([state_cconv[j], u_s], axis=1)[:, lq:])
            xf, xb = xs_matmul_ln(c, cc_w_pw2[j].astype(BF16), cc_b_pw2[j], xf, ln_g[i, 0], ln_b[i, 0],
                                  tm=LN_TM, tk=LN_TK)
        elif kind == 1:
            qkv_f, qkv_b = ws_matmul(xb, sb_w_qkv[j].astype(BF16)[None], n_out=3 * d, tm=TOK_TM, tn=1024,
                                     out_dtypes=(F32, BF16))
            k_p = qkv_f[:n_p, d:2 * d].reshape(1, bp, seq, SB_HEADS, SB_HEAD_DIM)
            v_p = qkv_f[:n_p, 2 * d:].reshape(1, bp, seq, SB_HEADS, SB_HEAD_DIM)
            k_s = qkv_f[n_p:, d:2 * d].reshape(1, bs, lq, SB_HEADS, SB_HEAD_DIM)
            v_s = qkv_f[n_p:, 2 * d:].reshape(1, bs, lq, SB_HEADS, SB_HEAD_DIM)
            o_p = sb_attention_prompt(qkv_b, tri, nbatch=bp, seq=seq, tq=512, tk=512)
            past = cache_sb_k.shape[2]
            o_s = sb_attention_sample(qkv_b, cache_sb_k[j].reshape(bs, past, d), cache_sb_v[j].reshape(bs, past, d),
                                      tri, row0=n_p, nbatch=bs, lq=lq, tk=512)
            o = jnp.concatenate([o_p, o_s], axis=0)
            xf, xb = xs_matmul_ln(o, sb_w_o[j].astype(BF16), zeros_d, xf, ln_g[i, 0], ln_b[i, 0],
                                  tm=LN_TM, tk=LN_TK)
        else:
            w_in = ssm_w_in[j]
            nzx = SSM_D_INNER + SSM_CONV_DIM
            (zx,) = ws_matmul(xb, w_in[:, :nzx].astype(BF16)[None], n_out=nzx, tm=TOK_TM, tn=1024,
                              out_dtypes=(F32,))
            w_dt = jnp.zeros((d, LANES), BF16).at[:, :SSM_HEADS].set(w_in[:, nzx:].astype(BF16))
            (dt_raw,) = ws_matmul(xb, w_dt[None], n_out=LANES, tm=TOK_TM, tn=LANES, out_dtypes=(F32,))
            xbc_s = zx[n_p:, SSM_D_INNER:].reshape(bs, lq, SSM_CONV_DIM)
            halo = 8
            hist_p = jnp.zeros((bp, halo, SSM_CONV_DIM), F32)
            hist_s = jnp.pad(state_ssm_conv[j], ((0, 0), (halo - (SSM_CONV - 1), 0), (0, 0)))
            conv = functools.partial(dwconv, zx, w=ssm_conv_w[j], cbias=ssm_conv_b[j], ln_g=None, ln_b=None,
                                     col0=SSM_D_INNER, taps=SSM_CONV, halo=halo, tc=2048, mode="silu",
                                     out_dtype=F32)
            a_p = conv(hist_p, row0=0, nseq=bp, length=seq, tl=256)
            a_s = conv(hist_s, row0=n_p, nseq=bs, length=lq, tl=lq)
            pad_s = SSD_Q - lq
            a_s = jnp.pad(a_s, ((0, 0), (0, pad_s), (0, 0)))
            z_p = zx
            z_s = jnp.pad(zx[n_p:, :SSM_D_INNER].reshape(bs, lq, SSM_D_INNER),
                          ((0, 0), (0, pad_s), (0, 0))).reshape(bs * SSD_Q, SSM_D_INNER)
            dt_p = dt_raw[:n_p].reshape(bp, seq, SSM_HPAD)
            dt_s = jnp.pad(dt_raw[n_p:].reshape(bs, lq, SSM_HPAD), ((0, 0), (0, pad_s), (0, 0)))
            st0_p = jnp.zeros((bp, SSM_D_STATE, SSM_D_INNER), F32)
            st0_s = jnp.swapaxes(state_ssm[j].reshape(bs, SSM_D_INNER, SSM_D_STATE), 1, 2)
            scan = functools.partial(ssd_scan, dt_bias=ssm_dt_bias[j], a_log=ssm_A_log[j], d_skip=ssm_D[j],
                                     norm_w=ssm_norm_w[j])
            y_p, st_p = scan(a_p, z_p, dt_p, st0_p, valid=seq)
            y_s, st_s = scan(a_s, z_s, dt_s, st0_s, valid=lq)
            y = jnp.concatenate([y_p.reshape(n_p, SSM_D_INNER), y_s[:, :lq].reshape(n_s, SSM_D_INNER)], axis=0)
            tail = SSM_CONV - 1
            new_sc_p = jnp.stack([zx[(b + 1) * seq - tail:(b + 1) * seq, SSM_D_INNER:] for b in range(bp)])[None]
            new_sc_s = jnp.concatenate([state_ssm_conv[j], xbc_s], axis=1)[:, lq:][None]
            unt = lambda s, n: jnp.swapaxes(s, 1, 2).reshape(1, n, SSM_HEADS, SSM_HEAD_DIM, SSM_D_STATE)
            new_ss_p = unt(st_p, bp)
            new_ss_s = unt(st_s, bs)
            xf, xb = xs_matmul_ln(y, ssm_w_out[j].astype(BF16), zeros_d, xf, ln_g[i, 0], ln_b[i, 0],
                                  tm=LN_TM, tk=LN_TK)
        f = i // 2
        if i % 2 == 0:
            (hmid,) = ws_matmul(xb, ffn_w_gu[f].astype(BF16)[None], n_out=FFN_DIM, tm=TOK_TM, tn=512,
                                act="swiglu", out_dtypes=(BF16,))
            xf, xb = xs_matmul_ln(hmid, ffn_w_down[f].astype(BF16), zeros_d, xf, ln_g[i, 1], ln_b[i, 1],
                                  tm=LN_TM, tk=LN_TK)
        else:
            xf, xb = moe_layer(xf, moe_w_router[f], moe_w_gu[f].astype(BF16), moe_w_down[f].astype(BF16),
                               ln_g[i, 1], ln_b[i, 1], tm_tok=LN_TM)

    y_prompt = xf[:n_p].reshape(bp, seq, d)
    y_sample = xf[n_p:].reshape(bs, lq, d)
    return (y_prompt, y_sample, jnp.stack(new_cc_p), jnp.stack(new_cc_s), k_p, v_p, k_s, v_s,
            new_sc_p, new_sc_s, new_ss_p, new_ss_s)
```
